```python
import jax, jax.numpy as jnp
from jax import lax
import numpy as np

D_MODEL = 1024
BATCH = 8
SEQ = 8192
DEPTH = 4
DEC_BATCH = 8
DEC_SEQ = 16
PAST_LEN = 1024

CHUNK = 64
N_BRANCH = 4
BRANCH_W = 512
A_CONV = 3
MLA_HEADS = 8
MLA_NOPE = 64
MLA_ROPE = 32
MLA_V = 64
Q_LORA = 384
KV_LORA = 256
ROPE_BASE = 10000.0
HG_HEADS = 4
HG_DK = 128
HG_DV = 128
D_CONV = 31
FFN_CONV = 3
D_FF = 2816
Q_BLOCK = 128
EPS = 1e-6
IN_COLS = 3 * BRANCH_W + Q_LORA + KV_LORA + MLA_ROPE + 4 * BRANCH_W + 2 * BRANCH_W + N_BRANCH * D_MODEL

kernel_name = 'hybrid_streaming_gated_branch_encoder_step'


def _rmsnorm(x, g):
    xf = x.astype(jnp.float32)
    y = xf * lax.rsqrt(jnp.mean(xf * xf, axis=-1, keepdims=True) + EPS)
    return (y * g.astype(jnp.float32)).astype(x.dtype)


def _layernorm(x, g, b):
    xf = x.astype(jnp.float32)
    mu = jnp.mean(xf, axis=-1, keepdims=True)
    xc = xf - mu
    y = xc * lax.rsqrt(jnp.mean(xc * xc, axis=-1, keepdims=True) + EPS)
    return (y * g.astype(jnp.float32) + b.astype(jnp.float32)).astype(x.dtype)


def _causal_dwconv(x, buf, w):
    K = w.shape[0]
    xp = jnp.concatenate([buf.astype(x.dtype), x], axis=1)
    y = lax.conv_general_dilated(xp, w[:, None, :].astype(x.dtype), window_strides=(1,), padding='VALID',
                                 dimension_numbers=('NWC', 'WIO', 'NWC'), feature_group_count=x.shape[-1])
    return y, xp[:, xp.shape[1] - (K - 1):]


def _rope(x, pos):
    half = MLA_ROPE // 2
    inv = ROPE_BASE ** (-jnp.arange(half, dtype=jnp.float32) / half)
    ang = pos.astype(jnp.float32)[:, None] * inv[None, :]
    shape = (1, ang.shape[0]) + (1,) * (x.ndim - 3) + (half,)
    cos = jnp.cos(ang).reshape(shape)
    sin = jnp.sin(ang).reshape(shape)
    x1 = x[..., :half].astype(jnp.float32)
    x2 = x[..., half:].astype(jnp.float32)
    return jnp.concatenate([x1 * cos - x2 * sin, x1 * sin + x2 * cos], axis=-1).astype(x.dtype)


def _mla_prompt(q_nope, q_rope, k_nope, k_rope, v):
    B, S, H, _ = q_nope.shape
    nb = S // Q_BLOCK
    scale = (MLA_NOPE + MLA_ROPE) ** -0.5
    key_chunk = jnp.arange(S) // CHUNK

    def blocks(a):
        return jnp.moveaxis(a.reshape((B, nb, Q_BLOCK) + a.shape[2:]), 1, 0)

    def one_block(args):
        i, qn, qr = args
        s = jnp.einsum('bqhd,bkhd->bhqk', qn, k_nope) + jnp.einsum('bqhd,bkd->bhqk', qr, k_rope)
        q_chunk = (i * Q_BLOCK + jnp.arange(Q_BLOCK)) // CHUNK
        mask = key_chunk[None, :] <= q_chunk[:, None]
        s = jnp.where(mask, s.astype(jnp.float32) * scale, -jnp.inf)
        p = jax.nn.softmax(s, axis=-1).astype(v.dtype)
        return jnp.einsum('bhqk,bkhd->bqhd', p, v)

    out = lax.map(one_block, (jnp.arange(nb), blocks(q_nope), blocks(q_rope)))
    return jnp.moveaxis(out, 0, 1).reshape(B, S, H, v.shape[-1])


def _mla_dense(q_nope, q_rope, k_nope, k_rope, v):
    scale = (MLA_NOPE + MLA_ROPE) ** -0.5
    s = jnp.einsum('bqhd,bkhd->bhqk', q_nope, k_nope) + jnp.einsum('bqhd,bkd->bhqk', q_rope, k_rope)
    p = jax.nn.softmax(s.astype(jnp.float32) * scale, axis=-1).astype(v.dtype)
    return jnp.einsum('bhqk,bkhd->bqhd', p, v)


def _hgrn2(q, k, v, logf, S0):
    B, T, H, DK = q.shape
    DV = v.shape[-1]
    C = min(CHUNK, T)
    n = T // C

    def chunks(a):
        return a.astype(jnp.float32).reshape(B, n, C, H, a.shape[-1]).transpose(1, 0, 3, 2, 4)

    mask = jnp.tril(jnp.ones((C, C), dtype=bool))[None, None, :, :, None]

    def step(S, inp):
        qc, kc, vc, gc = inp
        b = jnp.cumsum(gc, axis=2)
        o_inter = jnp.einsum('bhtk,bhkv->bhtv', qc * jnp.exp(b), S)
        diff = jnp.where(mask, b[:, :, :, None, :] - b[:, :, None, :, :], -jnp.inf)
        A = jnp.einsum('bhtk,bhtsk,bhsk->bhts', qc, jnp.exp(diff), kc)
        o = o_inter + jnp.einsum('bhts,bhsv->bhtv', A, vc)
        bl = b[:, :, -1:, :]
        S = jnp.exp(bl[:, :, 0, :])[..., None] * S + jnp.einsum('bhsk,bhsv->bhkv', kc * jnp.exp(bl - b), vc)
        return S, o

    S_fin, o = lax.scan(step, S0.astype(jnp.float32), (chunks(q), chunks(k), chunks(v), chunks(logf)))
    o = o.transpose(1, 0, 3, 2, 4).reshape(B, T, H, DV)
    return o.astype(q.dtype), S_fin.astype(S0.dtype)


def _split_cols(p):
    sizes = (BRANCH_W, BRANCH_W, BRANCH_W, Q_LORA, KV_LORA, MLA_ROPE,
             BRANCH_W, BRANCH_W, BRANCH_W, BRANCH_W, 2 * BRANCH_W, N_BRANCH * D_MODEL)
    outs = []
    off = 0
    for s in sizes:
        outs.append(p[..., off:off + s])
        off += s
    return outs


def _layer(x, pos, past, state, w, lb):
    (g_mix_pre, g_mix_post, g_ffn_pre, g_ffn_post, w_in, b_gate, conv_a_w,
     q_norm, w_uq, kv_norm, w_uk, w_uv, hg_norm, conv_d_w, conv_d_b, ln_d_g, ln_d_b,
     w_branch, w_out, w_up, ffn_conv_w, w_down) = w
    buf_a, S0, buf_d, buf_f = state
    B, T, _ = x.shape
    h = _rmsnorm(x, g_mix_pre)
    p = h @ w_in
    (a_b, a_c, a_x, cq, ckv, kr, hq, hf, hi, hg, d_in, gate_logits) = _split_cols(p)

    ya, nbuf_a = _causal_dwconv(a_c * a_x, buf_a, conv_a_w)
    out_a = a_b * ya

    q = (_rmsnorm(cq, q_norm) @ w_uq).reshape(B, T, MLA_HEADS, MLA_NOPE + MLA_ROPE)
    q_nope = q[..., :MLA_NOPE]
    q_rope = _rope(q[..., MLA_NOPE:], pos)
    ckv = _rmsnorm(ckv, kv_norm)
    kr = _rope(kr, pos)
    if past is None:
        ckv_all, kr_all = ckv, kr
    else:
        ckv_all = jnp.concatenate([past[0].astype(ckv.dtype), ckv], axis=1)
        kr_all = jnp.concatenate([past[1].astype(kr.dtype), kr], axis=1)
    n_keys = ckv_all.shape[1]
    k_nope = (ckv_all @ w_uk).reshape(B, n_keys, MLA_HEADS, MLA_NOPE)
    v = (ckv_all @ w_uv).reshape(B, n_keys, MLA_HEADS, MLA_V)
    if past is None:
        att = _mla_prompt(q_nope, q_rope, k_nope, kr_all, v)
    else:
        att = _mla_dense(q_nope, q_rope, k_nope, kr_all, v)
    out_b = att.reshape(B, T, MLA_HEADS * MLA_V)

    zf = hf.astype(jnp.float32)
    logf = jnp.logaddexp(jnp.log(lb), jnp.log1p(-lb) + jax.nn.log_sigmoid(zf))
    kk = (1.0 - lb) * jax.nn.sigmoid(-zf)
    o, S_new = _hgrn2(hq.reshape(B, T, HG_HEADS, HG_DK), kk.reshape(B, T, HG_HEADS, HG_DK),
                      hi.reshape(B, T, HG_HEADS, HG_DV), logf.reshape(B, T, HG_HEADS, HG_DK), S0)
    out_c = _rmsnorm(o, hg_norm.reshape(HG_HEADS, HG_DV)).reshape(B, T, HG_HEADS * HG_DV) * jax.nn.silu(hg)

    glu = d_in[..., :BRANCH_W] * jax.nn.sigmoid(d_in[..., BRANCH_W:])
    yd, nbuf_d = _causal_dwconv(glu, buf_d, conv_d_w)
    out_d = jax.nn.silu(_layernorm(yd + conv_d_b, ln_d_g, ln_d_b))

    gates = jax.nn.sigmoid(gate_logits + b_gate).reshape(B, T, N_BRANCH, D_MODEL)
    branches = (out_a, out_b, out_c, out_d)
    merged = gates[:, :, 0] * (branches[0] @ w_branch[0])
    for i in range(1, N_BRANCH):
        merged = merged + gates[:, :, i] * (branches[i] @ w_branch[i])
    x = x + _rmsnorm(merged @ w_out, g_mix_post)

    h2 = _rmsnorm(x, g_ffn_pre)
    gu = h2 @ w_up
    fg, nbuf_f = _causal_dwconv(gu[..., :D_FF], buf_f, ffn_conv_w)
    ff = (jax.nn.silu(fg) * gu[..., D_FF:]) @ w_down
    x = x + _rmsnorm(ff, g_ffn_post)
    return x, (ckv, kr, nbuf_a, S_new, nbuf_d, nbuf_f)


def setup_inputs(seed: int = 0) -> dict:
    key = jax.random.key(seed)
    ks = iter(jax.random.split(key, 40))

    def nrm(shape, scale):
        return scale * jax.random.normal(next(ks), shape, jnp.float32)

    def gain(shape):
        return 1.0 + nrm(shape, 0.02)

    return {
        'x_prompt': nrm((BATCH, SEQ, D_MODEL), 1.0),
        'x_sample': nrm((DEC_BATCH, DEC_SEQ, D_MODEL), 1.0),
        'cache_ckv': nrm((DEPTH, DEC_BATCH, PAST_LEN, KV_LORA), 1.0),
        'cache_krope': nrm((DEPTH, DEC_BATCH, PAST_LEN, MLA_ROPE), 1.0),
        'state_conv_a': nrm((DEPTH, DEC_BATCH, A_CONV - 1, BRANCH_W), 1.0),
        'state_hgrn': nrm((DEPTH, DEC_BATCH, HG_HEADS, HG_DK, HG_DV), 0.3),
        'state_conv_d': nrm((DEPTH, DEC_BATCH, D_CONV - 1, BRANCH_W), 1.0),
        'state_ffn_conv': nrm((DEPTH, DEC_BATCH, FFN_CONV - 1, D_FF), 1.0),
        'norm_mix_pre': gain((DEPTH, D_MODEL)),
        'norm_mix_post': gain((DEPTH, D_MODEL)),
        'norm_ffn_pre': gain((DEPTH, D_MODEL)),
        'norm_ffn_post': gain((DEPTH, D_MODEL)),
        'w_in': nrm((DEPTH, D_MODEL, IN_COLS), D_MODEL ** -0.5),
        'b_gate': nrm((DEPTH, N_BRANCH * D_MODEL), 0.1),
        'conv_a_w': nrm((DEPTH, A_CONV, BRANCH_W), 0.5),
        'mla_q_norm': gain((DEPTH, Q_LORA)),
        'mla_w_uq': nrm((DEPTH, Q_LORA, MLA_HEADS * (MLA_NOPE + MLA_ROPE)), Q_LORA ** -0.5),
        'mla_kv_norm': gain((DEPTH, KV_LORA)),
        'mla_w_uk': nrm((DEPTH, KV_LORA, MLA_HEADS * MLA_NOPE), KV_LORA ** -0.5),
        'mla_w_uv': nrm((DEPTH, KV_LORA, MLA_HEADS * MLA_V), KV_LORA ** -0.5),
        'hgrn_lb_raw': nrm((DEPTH, HG_HEADS * HG_DK), 0.1),
        'hgrn_norm': gain((DEPTH, HG_HEADS * HG_DV)),
        'conv_d_w': nrm((DEPTH, D_CONV, BRANCH_W), D_CONV ** -0.5),
        'conv_d_b': nrm((DEPTH, BRANCH_W), 0.02),
        'ln_d_g': gain((DEPTH, BRANCH_W)),
        'ln_d_b': nrm((DEPTH, BRANCH_W), 0.02),
        'w_branch': nrm((DEPTH, N_BRANCH, BRANCH_W, D_MODEL), BRANCH_W ** -0.5),
        'w_out': nrm((DEPTH, D_MODEL, D_MODEL), D_MODEL ** -0.5),
        'ffn_w_up': nrm((DEPTH, D_MODEL, 2 * D_FF), D_MODEL ** -0.5),
        'ffn_conv_w': nrm((DEPTH, FFN_CONV, D_FF), 0.5),
        'ffn_w_down': nrm((DEPTH, D_FF, D_MODEL), D_FF ** -0.5),
    }


def reference(x_prompt, x_sample, cache_ckv, cache_krope, state_conv_a, state_hgrn, state_conv_d, state_ffn_conv,
              norm_mix_pre, norm_mix_post, norm_ffn_pre, norm_ffn_post, w_in, b_gate, conv_a_w,
              mla_q_norm, mla_w_uq, mla_kv_norm, mla_w_uk, mla_w_uv, hgrn_lb_raw, hgrn_norm,
              conv_d_w, conv_d_b, ln_d_g, ln_d_b, w_branch, w_out, ffn_w_up, ffn_conv_w, ffn_w_down):
    lb_all = jnp.cumsum(jax.nn.softmax(hgrn_lb_raw.astype(jnp.float32), axis=0), axis=0)
    lb_all = lb_all - lb_all[:1]
    B, T = x_prompt.shape[0], x_prompt.shape[1]
    Ts = x_sample.shape[1]
    past_len = cache_ckv.shape[2]
    pos_p = jnp.arange(T)
    pos_s = past_len + jnp.arange(Ts)
    dt = x_prompt.dtype
    zero_state = (jnp.zeros((B, A_CONV - 1, BRANCH_W), dt), jnp.zeros((B, HG_HEADS, HG_DK, HG_DV), dt),
                  jnp.zeros((B, D_CONV - 1, BRANCH_W), dt), jnp.zeros((B, FFN_CONV - 1, D_FF), dt))
    yp, ys = x_prompt, x_sample
    p_states, s_states = [], []
    for l in range(DEPTH):
        w = (norm_mix_pre[l], norm_mix_post[l], norm_ffn_pre[l], norm_ffn_post[l], w_in[l], b_gate[l], conv_a_w[l],
             mla_q_norm[l], mla_w_uq[l], mla_kv_norm[l], mla_w_uk[l], mla_w_uv[l], hgrn_norm[l],
             conv_d_w[l], conv_d_b[l], ln_d_g[l], ln_d_b[l], w_branch[l], w_out[l],
             ffn_w_up[l], ffn_conv_w[l], ffn_w_down[l])
        yp, sp = _layer(yp, pos_p, None, zero_state, w, lb_all[l])
        ys, ss = _layer(ys, pos_s, (cache_ckv[l], cache_krope[l]),
                        (state_conv_a[l], state_hgrn[l], state_conv_d[l], state_ffn_conv[l]), w, lb_all[l])
        p_states.append(sp)
        s_states.append(ss)
    ps = [jnp.stack(z) for z in zip(*p_states)]
    sst = [jnp.stack(z) for z in zip(*s_states)]
    return (yp, ys, ps[0], ps[1], ps[2], ps[3], ps[4], ps[5], sst[0], sst[1], sst[2], sst[3], sst[4], sst[5])
```

```python
import functools

import numpy as np
import jax
import jax.numpy as jnp
from jax import lax
from jax.experimental import pallas as pl
from jax.experimental.pallas import tpu as pltpu

D_MODEL = 1024
BRANCH_W = 512
N_BRANCH = 4
MLA_HEADS = 8
MLA_NOPE = 64
MLA_ROPE = 32
MLA_V = 64
Q_LORA = 384
KV_LORA = 256
ROPE_BASE = 10000.0
HG_HEADS = 4
HG_D = 128
A_CONV = 3
D_CONV = 31
FFN_CONV = 3
D_FF = 2816
CHUNK = 64
EPS = 1e-6
HEAD_PAD = 128
SUB = 16

_C_A = 0
_C_CQ = 3 * BRANCH_W
_C_CKV = _C_CQ + Q_LORA
_C_HG = _C_CKV + KV_LORA
_C_DIN = _C_HG + 4 * BRANCH_W
_C_KR = _C_DIN + 2 * BRANCH_W
_C_END = _C_KR + 2 * MLA_ROPE

TILE_PROJ = 512
TILE_ATTN = 512
TILE_HGRN = 512
TILE_MERGE = 512
TILE_FFN = 512
TILE_KV = 512
VMEM_LIMIT = 56 * 1024 * 1024

F32 = jnp.float32
BF16 = jnp.bfloat16


def _rms(x, g):
    return x * lax.rsqrt(jnp.mean(x * x, axis=-1, keepdims=True) + EPS) * g


def _sigmoid(x):
    return 1.0 / (1.0 + jnp.exp(-x))


def _silu(x):
    return x * _sigmoid(x)


def _dot(a, b):
    return jnp.dot(a, b, preferred_element_type=F32)


def _const_spec(shape):
    nd = len(shape)
    return pl.BlockSpec(shape, lambda *_: (0,) * nd, pipeline_mode=pl.Buffered(1))


def _params(n_axes):
    return pltpu.CompilerParams(dimension_semantics=("arbitrary",) * n_axes,
                                vmem_limit_bytes=VMEM_LIMIT)


def _proj_kernel(x_ref, gpre_ref, w_ref, conva_ref, qn_ref, wuq_ref, kvn_ref,
                 cdw_ref, cdb_ref, lng_ref, lnb_ref, bufa_ref, bufd_ref,
                 cq_ref, sq_ref, ck_ref, sk_ref,
                 outa_ref, outd_ref, q_ref, ckv_ref, kr_ref,
                 hq_ref, hf_ref, hi_ref, hg_ref, nbufa_ref, nbufd_ref,
                 ua_s, gd_s, *, tm):
    t = pl.program_id(1)

    @pl.when(t == 0)
    def _():
        ua_s[0:8, :] = jnp.zeros((8, BRANCH_W), F32)
        ua_s[8 - (A_CONV - 1):8, :] = bufa_ref[0]
        gd_s[0:32, :] = jnp.zeros((32, BRANCH_W), F32)
        gd_s[32 - (D_CONV - 1):32, :] = bufd_ref[0]

    h = _rms(x_ref[0], gpre_ref[...]).astype(BF16)

    pa = _dot(h, w_ref[:, _C_A:_C_CQ])
    u = pa[:, BRANCH_W:2 * BRANCH_W] * pa[:, 2 * BRANCH_W:]
    ua_s[8:8 + tm, :] = u
    ya = (conva_ref[0:1, :] * ua_s[6:6 + tm, :] + conva_ref[1:2, :] * ua_s[7:7 + tm, :]
          + conva_ref[2:3, :] * u)
    outa_ref[0] = (pa[:, :BRANCH_W] * ya).astype(BF16)
    nbufa_ref[0] = ua_s[tm + 6:tm + 8, :]
    ua_s[0:8, :] = ua_s[tm:tm + 8, :]

    pd = _dot(h, w_ref[:, _C_DIN:_C_KR])
    gd_s[32:32 + tm, :] = pd[:, :BRANCH_W] * _sigmoid(pd[:, BRANCH_W:])
    yd = cdw_ref[0:1, :] * gd_s[2:2 + tm, :]
    for k in range(1, D_CONV):
        yd = yd + cdw_ref[k:k + 1, :] * gd_s[2 + k:2 + k + tm, :]
    yd = yd + cdb_ref[...]
    mu = jnp.mean(yd, axis=-1, keepdims=True)
    yc = yd - mu
    ln = yc * lax.rsqrt(jnp.mean(yc * yc, axis=-1, keepdims=True) + EPS) * lng_ref[...] + lnb_ref[...]
    outd_ref[0] = _silu(ln).astype(BF16)
    nbufd_ref[0] = gd_s[tm + 2:tm + 32, :]
    gd_s[0:32, :] = gd_s[tm:tm + 32, :]

    pq = _dot(h, w_ref[:, _C_CQ:_C_HG])
    cqn = _rms(pq[:, :Q_LORA], qn_ref[...]).astype(BF16)
    ckv_ref[0] = _rms(pq[:, Q_LORA:], kvn_ref[...])
    qab = _dot(cqn, wuq_ref[...])
    cq = cq_ref[...]
    sq = sq_ref[...]
    nq = MLA_HEADS * HEAD_PAD
    for hh in range(MLA_HEADS):
        lo = hh * HEAD_PAD
        q_ref[0, :, lo:lo + HEAD_PAD] = (qab[:, lo:lo + HEAD_PAD] * cq
                                         + qab[:, nq + lo:nq + lo + HEAD_PAD] * sq).astype(BF16)
    pk = _dot(h, w_ref[:, _C_KR:_C_END])
    kr_ref[0] = pk[:, :MLA_ROPE] * ck_ref[...] + pk[:, MLA_ROPE:] * sk_ref[...]

    ph = _dot(h, w_ref[:, _C_HG:_C_DIN])
    hq_ref[0] = ph[:, :BRANCH_W].astype(BF16)
    hf_ref[0] = ph[:, BRANCH_W:2 * BRANCH_W]
    hi_ref[0] = ph[:, 2 * BRANCH_W:3 * BRANCH_W].astype(BF16)
    hg_ref[0] = ph[:, 3 * BRANCH_W:].astype(BF16)


def _proj(x, gpre, w_small, conva, qn, wuq, kvn, cdw, cdb, lng, lnb, bufa, bufd, tabs, tm):
    B, T, _ = x.shape
    cq, sq, ck, sk = tabs
    nt = T // tm
    row = lambda w: pl.BlockSpec((1, tm, w), lambda b, t: (b, t, 0))
    tab = lambda w: pl.BlockSpec((tm, w), lambda b, t: (t, 0))
    st = lambda r: pl.BlockSpec((1, r, BRANCH_W), lambda b, t: (b, 0, 0))
    in_specs = [row(D_MODEL), _const_spec((1, D_MODEL)), _const_spec(w_small.shape),
                _const_spec(conva.shape), _const_spec((1, Q_LORA)), _const_spec(wuq.shape),
                _const_spec((1, KV_LORA)), _const_spec(cdw.shape), _const_spec((1, BRANCH_W)),
                _const_spec((1, BRANCH_W)), _const_spec((1, BRANCH_W)),
                st(A_CONV - 1), st(D_CONV - 1),
                tab(HEAD_PAD), tab(HEAD_PAD), tab(MLA_ROPE), tab(MLA_ROPE)]
    out_shape = [jax.ShapeDtypeStruct((B, T, BRANCH_W), BF16),
                 jax.ShapeDtypeStruct((B, T, BRANCH_W), BF16),
                 jax.ShapeDtypeStruct((B, T, MLA_HEADS * HEAD_PAD), BF16),
                 jax.ShapeDtypeStruct((B, T, KV_LORA), F32),
                 jax.ShapeDtypeStruct((B, T, MLA_ROPE), F32),
                 jax.ShapeDtypeStruct((B, T, BRANCH_W), BF16),
                 jax.ShapeDtypeStruct((B, T, BRANCH_W), F32),
                 jax.ShapeDtypeStruct((B, T, BRANCH_W), BF16),
                 jax.ShapeDtypeStruct((B, T, BRANCH_W), BF16),
                 jax.ShapeDtypeStruct((B, A_CONV - 1, BRANCH_W), F32),
                 jax.ShapeDtypeStruct((B, D_CONV - 1, BRANCH_W), F32)]
    out_specs = [row(BRANCH_W), row(BRANCH_W), row(MLA_HEADS * HEAD_PAD), row(KV_LORA), row(MLA_ROPE),
                 row(BRANCH_W), row(BRANCH_W), row(BRANCH_W), row(BRANCH_W),
                 st(A_CONV - 1), st(D_CONV - 1)]
    return pl.pallas_call(
        functools.partial(_proj_kernel, tm=tm),
        grid=(B, nt), in_specs=in_specs, out_specs=out_specs, out_shape=out_shape,
        scratch_shapes=[pltpu.VMEM((tm + 8, BRANCH_W), F32), pltpu.VMEM((tm + 32, BRANCH_W), F32)],
        compiler_params=_params(2), name="proj",
    )(x, gpre, w_small, conva, qn, wuq, kvn, cdw, cdb, lng, lnb, bufa, bufd, cq, sq, ck, sk)


def _kvexpand_kernel(ckv_ref, kr_ref, wk_ref, e_ref, wv_ref, k_ref, v_ref):
    c = ckv_ref[...].astype(BF16)
    k_ref[...] = (_dot(c, wk_ref[...]) + _dot(kr_ref[...].astype(BF16), e_ref[...])).astype(BF16)
    v_ref[...] = _dot(c, wv_ref[...]).astype(BF16)


def _kvexpand(ckv, kr, wk_pad, e_mat, wv, tm):
    n = ckv.shape[0]
    row = lambda w: pl.BlockSpec((tm, w), lambda i: (i, 0))
    return pl.pallas_call(
        _kvexpand_kernel, grid=(n // tm,),
        in_specs=[row(KV_LORA), row(MLA_ROPE), _const_spec(wk_pad.shape), _const_spec(e_mat.shape),
                  _const_spec(wv.shape)],
        out_specs=[row(MLA_HEADS * HEAD_PAD), row(MLA_HEADS * MLA_V)],
        out_shape=[jax.ShapeDtypeStruct((n, MLA_HEADS * HEAD_PAD), BF16),
                   jax.ShapeDtypeStruct((n, MLA_HEADS * MLA_V), BF16)],
        compiler_params=_params(1), name="kvexpand",
    )(ckv, kr, wk_pad, e_mat, wv)


def _attn_kernel(q_ref, k_ref, v_ref, o_ref, m_s, l_s, acc_s, *, causal, kv_valid, tq, tk):
    i = pl.program_id(1)
    j = pl.program_id(2)
    nkv = pl.num_programs(2)

    @pl.when(j == 0)
    def _():
        m_s[...] = jnp.full(m_s.shape, -jnp.inf, F32)
        l_s[...] = jnp.zeros(l_s.shape, F32)
        acc_s[...] = jnp.zeros(acc_s.shape, F32)

    def step(mask):
        for hh in range(MLA_HEADS):
            qh = q_ref[0, :, hh * HEAD_PAD:(hh + 1) * HEAD_PAD]
            kh = k_ref[0, :, hh * HEAD_PAD:(hh + 1) * HEAD_PAD]
            s = lax.dot_general(qh, kh, (((1,), (1,)), ((), ())), preferred_element_type=F32)
            if mask is not None:
                s = jnp.where(mask, s, -jnp.inf)
            m_prev = m_s[hh]
            m_new = jnp.maximum(m_prev, jnp.max(s, axis=-1, keepdims=True))
            alpha = jnp.exp(m_prev - m_new)
            p = jnp.exp(s - m_new)
            l_s[hh] = alpha * l_s[hh] + jnp.sum(p, axis=-1, keepdims=True)
            acc_s[hh] = alpha * acc_s[hh] + _dot(p.astype(BF16), v_ref[0, :, hh * MLA_V:(hh + 1) * MLA_V])
            m_s[hh] = m_new

    if causal:
        @pl.when(j < i)
        def _():
            step(None)

        @pl.when(j == i)
        def _():
            qc = lax.broadcasted_iota(jnp.int32, (tq, tk), 0) // CHUNK
            kc = lax.broadcasted_iota(jnp.int32, (tq, tk), 1) // CHUNK
            step(kc <= qc)
        last = j == i
    else:
        if kv_valid < tk:
            step(lax.broadcasted_iota(jnp.int32, (tq, tk), 1) < kv_valid)
        else:
            step(None)
        last = j == nkv - 1

    @pl.when(last)
    def _():
        for hh in range(MLA_HEADS):
            o_ref[0, :, hh * MLA_V:(hh + 1) * MLA_V] = (acc_s[hh] / l_s[hh]).astype(BF16)


def _attn(q, k, v, causal, kv_valid, tq, tk):
    B, T, _ = q.shape
    nq = T // tq
    nkv = k.shape[1] // tk
    if causal:
        kv_idx = lambda b, i, j: (b, jnp.minimum(j, i), 0)
    else:
        kv_idx = lambda b, i, j: (b, j, 0)
    return pl.pallas_call(
        functools.partial(_attn_kernel, causal=causal, kv_valid=kv_valid, tq=tq, tk=tk),
        grid=(B, nq, nkv),
        in_specs=[pl.BlockSpec((1, tq, MLA_HEADS * HEAD_PAD), lambda b, i, j: (b, i, 0)),
                  pl.BlockSpec((1, tk, MLA_HEADS * HEAD_PAD), kv_idx),
                  pl.BlockSpec((1, tk, MLA_HEADS * MLA_V), kv_idx)],
        out_specs=pl.BlockSpec((1, tq, MLA_HEADS * MLA_V), lambda b, i, j: (b, i, 0)),
        out_shape=jax.ShapeDtypeStruct((B, T, MLA_HEADS * MLA_V), BF16),
        scratch_shapes=[pltpu.VMEM((MLA_HEADS, tq, 1), F32), pltpu.VMEM((MLA_HEADS, tq, 1), F32),
                        pltpu.VMEM((MLA_HEADS, tq, MLA_V), F32)],
        compiler_params=_params(3), name="attn",
    )(q, k, v)


def _hgrn_kernel(hq_ref, hf_ref, hi_ref, hg_ref, lbraw_ref, gn_ref, s0_ref,
                 o_ref, sfin_ref, st_s, *, layer, tm, cc):
    t = pl.program_id(1)
    nt = pl.num_programs(1)
    W = HG_HEADS * HG_D

    @pl.when(t == 0)
    def _():
        for hh in range(HG_HEADS):
            st_s[hh] = s0_ref[0, hh].T

    raw = lbraw_ref[...]
    e = jnp.exp(raw - jnp.max(raw, axis=0, keepdims=True))
    pr = e / jnp.sum(e, axis=0, keepdims=True)
    cum = pr[0:1, :]
    for r in range(1, layer + 1):
        cum = cum + pr[r:r + 1, :]
    lb = cum - pr[0:1, :]
    log_lb = jnp.log(lb)
    log_1mlb = jnp.log1p(-lb)
    gn = gn_ref[...]

    tri = (lax.broadcasted_iota(jnp.int32, (cc, cc), 0)
           >= lax.broadcasted_iota(jnp.int32, (cc, cc), 1)).astype(BF16)
    row_c = lax.broadcasted_iota(jnp.int32, (cc, 1), 0)
    lane_c = lax.broadcasted_iota(jnp.int32, (SUB, cc), 1)
    row_s = lax.broadcasted_iota(jnp.int32, (SUB, 1), 0)
    nsub = cc // SUB

    def chunk(c, carry):
        r0 = pl.multiple_of(c * cc, cc)
        zf = hf_ref[0, pl.ds(r0, cc), :]
        q = hq_ref[0, pl.ds(r0, cc), :].astype(F32)
        v = hi_ref[0, pl.ds(r0, cc), :]
        ls = jnp.minimum(zf, 0.0) - jnp.log1p(jnp.exp(-jnp.abs(zf)))
        bb = log_1mlb + ls
        g = jnp.maximum(log_lb, bb) + jnp.log1p(jnp.exp(-jnp.abs(log_lb - bb)))
        kk = (1.0 - lb) * _sigmoid(-zf)
        g1 = g.astype(BF16)
        r1 = g - g1.astype(F32)
        g2 = r1.astype(BF16)
        g3 = (r1 - g2.astype(F32)).astype(BF16)
        b = _dot(tri, g1) + _dot(tri, g2) + _dot(tri, g3)
        bl = b[cc - 1:cc, :]
        qb = (q * jnp.exp(b)).astype(BF16)
        kdec = (kk * jnp.exp(bl - b)).astype(BF16)
        ebl = jnp.exp(bl)

        a_rows = []
        for si in range(nsub):
            lo = si * SUB
            q_i = q[lo:lo + SUB, :]
            b_i = b[lo:lo + SUB, :]
            if si > 0:
                ref = b[lo - 1:lo, :]
                qt = (q_i * jnp.exp(b_i - ref)).astype(BF16)
                kt = (kk * jnp.exp(jnp.where(row_c < lo, ref - b, -jnp.inf))).astype(BF16)
            a_h = []
            for hh in range(HG_HEADS):
                sl = slice(hh * HG_D, (hh + 1) * HG_D)
                if si > 0:
                    a = lax.dot_general(qt[:, sl], kt[:, sl], (((1,), (1,)), ((), ())),
                                        preferred_element_type=F32)
                else:
                    a = jnp.zeros((SUB, cc), F32)
                a_h.append(a)
            for s in range(SUB):
                b_s = b[lo + s:lo + s + 1, :]
                k_s = kk[lo + s:lo + s + 1, :]
                z = q_i * k_s * jnp.exp(jnp.where(row_s >= s, b_i - b_s, -jnp.inf))
                for hh in range(HG_HEADS):
                    col = jnp.sum(z[:, hh * HG_D:(hh + 1) * HG_D], axis=-1, keepdims=True)
                    a_h[hh] = jnp.where(lane_c == lo + s, col, a_h[hh])
            a_rows.append(a_h)

        outs = []
        for hh in range(HG_HEADS):
            sl = slice(hh * HG_D, (hh + 1) * HG_D)
            a_full = jnp.concatenate([a_rows[si][hh] for si in range(nsub)], axis=0).astype(BF16)
            st = st_s[hh]
            o = lax.dot_general(qb[:, sl], st.astype(BF16), (((1,), (1,)), ((), ())),
                                preferred_element_type=F32)
            o = o + _dot(a_full, v[:, sl])
            st_s[hh] = st * ebl[:, sl] + lax.dot_general(v[:, sl], kdec[:, sl], (((0,), (0,)), ((), ())),
                                                         preferred_element_type=F32)
            outs.append(o * lax.rsqrt(jnp.mean(o * o, axis=-1, keepdims=True) + EPS))
        on = jnp.concatenate(outs, axis=-1) * gn
        o_ref[0, pl.ds(r0, cc), :] = (on * _silu(hg_ref[0, pl.ds(r0, cc), :].astype(F32))).astype(BF16)
        return carry

    lax.fori_loop(0, tm // cc, chunk, 0)

    @pl.when(t == nt - 1)
    def _():
        for hh in range(HG_HEADS):
            sfin_ref[0, hh] = st_s[hh].T


def _hgrn(hq, hf, hi, hg, lb_raw, gn, s0, layer, tm, cc):
    B, T, W = hq.shape
    row = pl.BlockSpec((1, tm, W), lambda b, t: (b, t, 0))
    st = pl.BlockSpec((1, HG_HEADS, HG_D, HG_D), lambda b, t: (b, 0, 0, 0))
    return pl.pallas_call(
        functools.partial(_hgrn_kernel, layer=layer, tm=tm, cc=cc),
        grid=(B, T // tm),
        in_specs=[row, row, row, row, _const_spec(lb_raw.shape), _const_spec((1, W)), st],
        out_specs=[row, st],
        out_shape=[jax.ShapeDtypeStruct((B, T, W), BF16),
                   jax.ShapeDtypeStruct((B, HG_HEADS, HG_D, HG_D), F32)],
        scratch_shapes=[pltpu.VMEM((HG_HEADS, HG_D, HG_D), F32)],
        compiler_params=_params(2), name="hgrn",
    )(hq, hf, hi, hg, lb_raw, gn, s0)


def _merge_kernel(x_ref, a_ref, b_ref, c_ref, d_ref, gpre_ref, wg_ref, bg_ref, wb_ref, wo_ref, gpost_ref,
                  y_ref):
    x = x_ref[...]
    h = _rms(x, gpre_ref[...]).astype(BF16)
    merged = None
    for i, br in enumerate((a_ref, b_ref, c_ref, d_ref)):
        gate = _sigmoid(_dot(h, wg_ref[:, i * D_MODEL:(i + 1) * D_MODEL])
                        + bg_ref[:, i * D_MODEL:(i + 1) * D_MODEL])
        term = gate * _dot(br[...], wb_ref[i])
        merged = term if merged is None else merged + term
    y = _dot(merged.astype(BF16), wo_ref[...])
    y_ref[...] = x + _rms(y, gpost_ref[...])


def _merge(x, br, gpre, wg, bg, wb, wo, gpost, tm):
    n = x.shape[0]
    row = lambda w: pl.BlockSpec((tm, w), lambda i: (i, 0))
    return pl.pallas_call(
        _merge_kernel, grid=(n // tm,),
        in_specs=[row(D_MODEL)] + [row(BRANCH_W)] * 4
                 + [_const_spec((1, D_MODEL)), _const_spec(wg.shape), _const_spec(bg.shape),
                    _const_spec(wb.shape), _const_spec(wo.shape), _const_spec((1, D_MODEL))],
        out_specs=row(D_MODEL),
        out_shape=jax.ShapeDtypeStruct((n, D_MODEL), F32),
        compiler_params=_params(1), name="merge",
    )(x, *br, gpre, wg, bg, wb, wo, gpost)


def _ffn_kernel(x_ref, gpre_ref, wup_ref, cw_ref, wdn_ref, gpost_ref, buf_ref,
                y_ref, nbuf_ref, g_s, *, tm, nsplit):
    t = pl.program_id(1)

    @pl.when(t == 0)
    def _():
        g_s[0:8, :] = jnp.zeros((8, D_FF), F32)
        g_s[8 - (FFN_CONV - 1):8, :] = buf_ref[0]

    x = x_ref[0]
    h = _rms(x, gpre_ref[...]).astype(BF16)
    wc = D_FF // nsplit
    ff = None
    for s in range(nsplit):
        lo = s * wc
        g = _dot(h, wup_ref[:, lo:lo + wc])
        g_s[8:8 + tm, lo:lo + wc] = g
        fg = (cw_ref[0:1, lo:lo + wc] * g_s[6:6 + tm, lo:lo + wc]
              + cw_ref[1:2, lo:lo + wc] * g_s[7:7 + tm, lo:lo + wc]
              + cw_ref[2:3, lo:lo + wc] * g)
        up = _dot(h, wup_ref[:, D_FF + lo:D_FF + lo + wc])
        part = _dot((_silu(fg) * up).astype(BF16), wdn_ref[lo:lo + wc, :])
        ff = part if ff is None else ff + part
    y_ref[0] = x + _rms(ff, gpost_ref[...])
    nbuf_ref[0] = g_s[tm + 6:tm + 8, :]
    g_s[0:8, :] = g_s[tm:tm + 8, :]


def _ffn(x, gpre, wup, cw, wdn, gpost, buf, tm):
    B, T, _ = x.shape
    row = pl.BlockSpec((1, tm, D_MODEL), lambda b, t: (b, t, 0))
    st = pl.BlockSpec((1, FFN_CONV - 1, D_FF), lambda b, t: (b, 0, 0))
    return pl.pallas_call(
        functools.partial(_ffn_kernel, tm=tm, nsplit=2),
        grid=(B, T // tm),
        in_specs=[row, _const_spec((1, D_MODEL)), _const_spec(wup.shape), _const_spec(cw.shape),
                  _const_spec(wdn.shape), _const_spec((1, D_MODEL)), st],
        out_specs=[row, st],
        out_shape=[jax.ShapeDtypeStruct((B, T, D_MODEL), F32),
                   jax.ShapeDtypeStruct((B, FFN_CONV - 1, D_FF), F32)],
        scratch_shapes=[pltpu.VMEM((tm + 8, D_FF), F32)],
        compiler_params=_params(2), name="ffn",
    )(x, gpre, wup, cw, wdn, gpost, buf)


def _tile(n, pref):
    return pref if n % pref == 0 else n


def _rope_tables(pos):
    half = MLA_ROPE // 2
    inv = ROPE_BASE ** (-jnp.arange(half, dtype=F32) / half)
    ang = pos.astype(F32)[:, None] * inv[None, :]
    cos, sin = jnp.cos(ang), jnp.sin(ang)
    n = pos.shape[0]
    scale = (MLA_NOPE + MLA_ROPE) ** -0.5
    pad = jnp.zeros((n, HEAD_PAD - MLA_NOPE - MLA_ROPE), F32)
    cq = scale * jnp.concatenate([jnp.ones((n, MLA_NOPE), F32), cos, cos, pad], axis=1)
    sq = scale * jnp.concatenate([jnp.zeros((n, MLA_NOPE), F32), -sin, sin, pad], axis=1)
    ck = jnp.concatenate([cos, cos], axis=1)
    sk = jnp.concatenate([-sin, sin], axis=1)
    return cq, sq, ck, sk


def _layer_weights(l, norm_mix_pre, norm_mix_post, norm_ffn_pre, norm_ffn_post, w_in, b_gate, conv_a_w,
                   mla_q_norm, mla_w_uq, mla_kv_norm, mla_w_uk, mla_w_uv, hgrn_norm,
                   conv_d_w, conv_d_b, ln_d_g, ln_d_b, w_branch, w_out, ffn_w_up, ffn_conv_w, ffn_w_down):
    wi = w_in[l]
    o_kr = 3 * BRANCH_W + Q_LORA + KV_LORA
    o_h = o_kr + MLA_ROPE
    o_g = o_h + 6 * BRANCH_W
    half = MLA_ROPE // 2
    w_small = jnp.concatenate(
        [wi[:, :o_kr], wi[:, o_h:o_g], wi[:, o_kr:o_h], wi[:, o_kr + half:o_h], wi[:, o_kr:o_kr + half]],
        axis=1).astype(BF16)
    w_gate = wi[:, o_g:].astype(BF16)
    uq = mla_w_uq[l].reshape(Q_LORA, MLA_HEADS, MLA_NOPE + MLA_ROPE)
    nope, r1, r2 = uq[..., :MLA_NOPE], uq[..., MLA_NOPE:MLA_NOPE + half], uq[..., MLA_NOPE + half:]
    z_pad = jnp.zeros((Q_LORA, MLA_HEADS, HEAD_PAD - MLA_NOPE - MLA_ROPE), F32)
    z_nope = jnp.zeros((Q_LORA, MLA_HEADS, MLA_NOPE), F32)
    wuq = jnp.concatenate(
        [jnp.concatenate([nope, r1, r2, z_pad], axis=-1).reshape(Q_LORA, MLA_HEADS * HEAD_PAD),
         jnp.concatenate([z_nope, r2, r1, z_pad], axis=-1).reshape(Q_LORA, MLA_HEADS * HEAD_PAD)],
        axis=1).astype(BF16)
    uk = mla_w_uk[l].reshape(KV_LORA, MLA_HEADS, MLA_NOPE)
    wk_pad = jnp.concatenate([uk, jnp.zeros((KV_LORA, MLA_HEADS, HEAD_PAD - MLA_NOPE), F32)],
                             axis=-1).reshape(KV_LORA, MLA_HEADS * HEAD_PAD).astype(BF16)
    e_np = np.zeros((MLA_ROPE, MLA_HEADS, HEAD_PAD), np.float32)
    for r in range(MLA_ROPE):
        e_np[r, :, MLA_NOPE + r] = 1.0
    e_mat = jnp.asarray(e_np.reshape(MLA_ROPE, MLA_HEADS * HEAD_PAD), BF16)
    r2d = lambda a: a[l].reshape(1, -1)
    return dict(
        gpre=r2d(norm_mix_pre), gpost=r2d(norm_mix_post), fpre=r2d(norm_ffn_pre), fpost=r2d(norm_ffn_post),
        w_small=w_small, w_gate=w_gate, b_gate=r2d(b_gate), conva=conv_a_w[l], qn=r2d(mla_q_norm), wuq=wuq,
        kvn=r2d(mla_kv_norm), wk_pad=wk_pad, e_mat=e_mat, wv=mla_w_uv[l].astype(BF16), gn=r2d(hgrn_norm),
        cdw=conv_d_w[l], cdb=r2d(conv_d_b), lng=r2d(ln_d_g), lnb=r2d(ln_d_b),
        wb=w_branch[l].astype(BF16), wo=w_out[l].astype(BF16), wup=ffn_w_up[l].astype(BF16),
        cw=ffn_conv_w[l], wdn=ffn_w_down[l].astype(BF16))


def _layer(x, tabs, past, state, w, lb_raw, layer):
    B, T, _ = x.shape
    buf_a, s0, buf_d, buf_f = state
    (out_a, out_d, q, ckv, kr, hq, hf, hi, hg, nbuf_a, nbuf_d) = _proj(
        x, w["gpre"], w["w_small"], w["conva"], w["qn"], w["wuq"], w["kvn"], w["cdw"], w["cdb"],
        w["lng"], w["lnb"], buf_a, buf_d, tabs, _tile(T, TILE_PROJ))

    if past is None:
        ckv_all, kr_all, kv_valid = ckv, kr, T
    else:
        n_keys = past[0].shape[1] + T
        n_pad = -n_keys % 128
        ckv_all = jnp.concatenate([past[0], ckv, jnp.zeros((B, n_pad, KV_LORA), F32)], axis=1)
        kr_all = jnp.concatenate([past[1], kr, jnp.zeros((B, n_pad, MLA_ROPE), F32)], axis=1)
        kv_valid = n_keys
    nk = ckv_all.shape[1]
    kcat, v = _kvexpand(ckv_all.reshape(B * nk, KV_LORA), kr_all.reshape(B * nk, MLA_ROPE),
                        w["wk_pad"], w["e_mat"], w["wv"], _tile(B * nk, TILE_KV))
    kcat = kcat.reshape(B, nk, -1)
    v = v.reshape(B, nk, -1)
    if past is None:
        ta = _tile(T, TILE_ATTN)
        out_b = _attn(q, kcat, v, True, kv_valid, ta, ta)
    else:
        out_b = _attn(q, kcat, v, False, kv_valid, T, nk)

    cc = min(CHUNK, T)
    out_c, s_new = _hgrn(hq, hf, hi, hg, lb_raw, w["gn"], s0, layer, _tile(T, TILE_HGRN), cc)

    flat = lambda a: a.reshape(B * T, a.shape[-1])
    x2 = _merge(flat(x), [flat(out_a), flat(out_b), flat(out_c), flat(out_d)], w["gpre"], w["w_gate"],
                w["b_gate"], w["wb"], w["wo"], w["gpost"], _tile(B * T, TILE_MERGE))
    x3, nbuf_f = _ffn(x2.reshape(B, T, D_MODEL), w["fpre"], w["wup"], w["cw"], w["wdn"], w["fpost"], buf_f,
                      _tile(T, TILE_FFN))
    return x3, (ckv, kr, nbuf_a, s_new, nbuf_d, nbuf_f)


def kernel(x_prompt, x_sample, cache_ckv, cache_krope, state_conv_a, state_hgrn, state_conv_d, state_ffn_conv,
           norm_mix_pre, norm_mix_post, norm_ffn_pre, norm_ffn_post, w_in, b_gate, conv_a_w,
           mla_q_norm, mla_w_uq, mla_kv_norm, mla_w_uk, mla_w_uv, hgrn_lb_raw, hgrn_norm,
           conv_d_w, conv_d_b, ln_d_g, ln_d_b, w_branch, w_out, ffn_w_up, ffn_conv_w, ffn_w_down):
    depth = w_in.shape[0]
    B, T, _ = x_prompt.shape
    Bs, Ts, _ = x_sample.shape
    past_len = cache_ckv.shape[2]
    tabs_p = _rope_tables(jnp.arange(T))
    tabs_s = _rope_tables(past_len + jnp.arange(Ts))
    zero_state = (jnp.zeros((B, A_CONV - 1, BRANCH_W), F32), jnp.zeros((B, HG_HEADS, HG_D, HG_D), F32),
                  jnp.zeros((B, D_CONV - 1, BRANCH_W), F32), jnp.zeros((B, FFN_CONV - 1, D_FF), F32))
    lb_raw = hgrn_lb_raw.astype(F32)
    yp, ys = x_prompt, x_sample
    p_states, s_states = [], []
    for l in range(depth):
        w = _layer_weights(l, norm_mix_pre, norm_mix_post, norm_ffn_pre, norm_ffn_post, w_in, b_gate, conv_a_w,
                           mla_q_norm, mla_w_uq, mla_kv_norm, mla_w_uk, mla_w_uv, hgrn_norm,
                           conv_d_w, conv_d_b, ln_d_g, ln_d_b, w_branch, w_out, ffn_w_up, ffn_conv_w,
                           ffn_w_down)
        yp, sp = _layer(yp, tabs_p, None, zero_state, w, lb_raw, l)
        ys, ss = _layer(ys, tabs_s, (cache_ckv[l], cache_krope[l]),
                        (state_conv_a[l], state_hgrn[l], state_conv_d[l], state_ffn_conv[l]), w, lb_raw, l)
        p_states.append(sp)
        s_states.append(ss)
    ps = [jnp.stack(z) for z in zip(*p_states)]
    sst = [jnp.stack(z) for z in zip(*s_states)]
    return (yp, ys, ps[0], ps[1], ps[2], ps[3], ps[4], ps[5], sst[0], sst[1], sst[2], sst[3], sst[4], sst[5])
```

```python
import functools

import numpy as np
import jax
import jax.numpy as jnp
from jax import lax
from jax.experimental import pallas as pl
from jax.experimental.pallas import tpu as pltpu

D_MODEL = 1024
BRANCH_W = 512
N_BRANCH = 4
MLA_HEADS = 8
MLA_NOPE = 64
MLA_ROPE = 32
MLA_V = 64
Q_LORA = 384
KV_LORA = 256
ROPE_BASE = 10000.0
HG_HEADS = 4
HG_D = 128
A_CONV = 3
D_CONV = 31
FFN_CONV = 3
D_FF = 2816
CHUNK = 64
EPS = 1e-6
HEAD_PAD = 128
SUB = 16

_C_A = 0
_C_CQ = 3 * BRANCH_W
_C_CKV = _C_CQ + Q_LORA
_C_HG = _C_CKV + KV_LORA
_C_DIN = _C_HG + 4 * BRANCH_W
_C_KR = _C_DIN + 2 * BRANCH_W
_C_END = _C_KR + 2 * MLA_ROPE

TILE_PROJ = 512
TILE_ATTN_Q = 512
TILE_ATTN_K = 512
TILE_HGRN = 512
TILE_MERGE = 512
TILE_FFN = 512
VMEM_LIMIT = 56 * 1024 * 1024

F32 = jnp.float32
BF16 = jnp.bfloat16


def _rms(x, g):
    return x * lax.rsqrt(jnp.mean(x * x, axis=-1, keepdims=True) + EPS) * g


def _sigmoid(x):
    return 1.0 / (1.0 + jnp.exp(-x))


def _silu(x):
    return x * _sigmoid(x)


def _dot(a, b):
    return jnp.dot(a, b, preferred_element_type=F32)


def _const_spec(shape):
    nd = len(shape)
    return pl.BlockSpec(shape, lambda *_: (0,) * nd, pipeline_mode=pl.Buffered(1))


def _params(n_axes):
    return pltpu.CompilerParams(dimension_semantics=("arbitrary",) * n_axes,
                                vmem_limit_bytes=VMEM_LIMIT)


def _proj_kernel(x_ref, gpre_ref, w_ref, conva_ref, qn_ref, wuq_ref, kvn_ref,
                 cdw_ref, cdb_ref, lng_ref, lnb_ref, bufa_ref, bufd_ref,
                 cq_ref, sq_ref, ck_ref, sk_ref,
                 outa_ref, outd_ref, q_ref, ckv_ref, kr_ref,
                 hq_ref, hf_ref, hi_ref, hg_ref, nbufa_ref, nbufd_ref,
                 ua_s, gd_s, *, tm):
    t = pl.program_id(1)

    @pl.when(t == 0)
    def _():
        ua_s[0:8, :] = jnp.zeros((8, BRANCH_W), F32)
        ua_s[8 - (A_CONV - 1):8, :] = bufa_ref[0]
        gd_s[0:32, :] = jnp.zeros((32, BRANCH_W), F32)
        gd_s[32 - (D_CONV - 1):32, :] = bufd_ref[0]

    h = _rms(x_ref[0], gpre_ref[...]).astype(BF16)

    pa = _dot(h, w_ref[:, _C_A:_C_CQ])
    u = pa[:, BRANCH_W:2 * BRANCH_W] * pa[:, 2 * BRANCH_W:]
    ua_s[8:8 + tm, :] = u
    ya = (conva_ref[0:1, :] * ua_s[6:6 + tm, :] + conva_ref[1:2, :] * ua_s[7:7 + tm, :]
          + conva_ref[2:3, :] * u)
    outa_ref[0] = (pa[:, :BRANCH_W] * ya).astype(BF16)
    nbufa_ref[0] = ua_s[tm + 6:tm + 8, :]
    ua_s[0:8, :] = ua_s[tm:tm + 8, :]

    pd = _dot(h, w_ref[:, _C_DIN:_C_KR])
    gd_s[32:32 + tm, :] = pd[:, :BRANCH_W] * _sigmoid(pd[:, BRANCH_W:])
    yd = cdw_ref[0:1, :] * gd_s[2:2 + tm, :]
    for k in range(1, D_CONV):
        yd = yd + cdw_ref[k:k + 1, :] * gd_s[2 + k:2 + k + tm, :]
    yd = yd + cdb_ref[...]
    mu = jnp.mean(yd, axis=-1, keepdims=True)
    yc = yd - mu
    ln = yc * lax.rsqrt(jnp.mean(yc * yc, axis=-1, keepdims=True) + EPS) * lng_ref[...] + lnb_ref[...]
    outd_ref[0] = _silu(ln).astype(BF16)
    nbufd_ref[0] = gd_s[tm + 2:tm + 32, :]
    gd_s[0:32, :] = gd_s[tm:tm + 32, :]

    pq = _dot(h, w_ref[:, _C_CQ:_C_HG])
    cqn = _rms(pq[:, :Q_LORA], qn_ref[...]).astype(BF16)
    ckv_ref[0] = _rms(pq[:, Q_LORA:], kvn_ref[...])
    qab = _dot(cqn, wuq_ref[...])
    cq = cq_ref[...]
    sq = sq_ref[...]
    nq = MLA_HEADS * HEAD_PAD
    for hh in range(MLA_HEADS):
        lo = hh * HEAD_PAD
        q_ref[0, :, lo:lo + HEAD_PAD] = (qab[:, lo:lo + HEAD_PAD] * cq
                                         + qab[:, nq + lo:nq + lo + HEAD_PAD] * sq).astype(BF16)
    pk = _dot(h, w_ref[:, _C_KR:_C_END])
    kr_ref[0] = pk[:, :MLA_ROPE] * ck_ref[...] + pk[:, MLA_ROPE:] * sk_ref[...]

    ph = _dot(h, w_ref[:, _C_HG:_C_DIN])
    hq_ref[0] = ph[:, :BRANCH_W].astype(BF16)
    hf_ref[0] = ph[:, BRANCH_W:2 * BRANCH_W]
    hi_ref[0] = ph[:, 2 * BRANCH_W:3 * BRANCH_W].astype(BF16)
    hg_ref[0] = ph[:, 3 * BRANCH_W:].astype(BF16)


def _proj(x, gpre, w_small, conva, qn, wuq, kvn, cdw, cdb, lng, lnb, bufa, bufd, tabs, tm):
    B, T, _ = x.shape
    cq, sq, ck, sk = tabs
    nt = T // tm
    row = lambda w: pl.BlockSpec((1, tm, w), lambda b, t: (b, t, 0))
    tab = lambda w: pl.BlockSpec((tm, w), lambda b, t: (t, 0))
    st = lambda r: pl.BlockSpec((1, r, BRANCH_W), lambda b, t: (b, 0, 0))
    in_specs = [row(D_MODEL), _const_spec((1, D_MODEL)), _const_spec(w_small.shape),
                _const_spec(conva.shape), _const_spec((1, Q_LORA)), _const_spec(wuq.shape),
                _const_spec((1, KV_LORA)), _const_spec(cdw.shape), _const_spec((1, BRANCH_W)),
                _const_spec((1, BRANCH_W)), _const_spec((1, BRANCH_W)),
                st(A_CONV - 1), st(D_CONV - 1),
                tab(HEAD_PAD), tab(HEAD_PAD), tab(MLA_ROPE), tab(MLA_ROPE)]
    out_shape = [jax.ShapeDtypeStruct((B, T, BRANCH_W), BF16),
                 jax.ShapeDtypeStruct((B, T, BRANCH_W), BF16),
                 jax.ShapeDtypeStruct((B, T, MLA_HEADS * HEAD_PAD), BF16),
                 jax.ShapeDtypeStruct((B, T, KV_LORA), F32),
                 jax.ShapeDtypeStruct((B, T, MLA_ROPE), F32),
                 jax.ShapeDtypeStruct((B, T, BRANCH_W), BF16),
                 jax.ShapeDtypeStruct((B, T, BRANCH_W), F32),
                 jax.ShapeDtypeStruct((B, T, BRANCH_W), BF16),
                 jax.ShapeDtypeStruct((B, T, BRANCH_W), BF16),
                 jax.ShapeDtypeStruct((B, A_CONV - 1, BRANCH_W), F32),
                 jax.ShapeDtypeStruct((B, D_CONV - 1, BRANCH_W), F32)]
    out_specs = [row(BRANCH_W), row(BRANCH_W), row(MLA_HEADS * HEAD_PAD), row(KV_LORA), row(MLA_ROPE),
                 row(BRANCH_W), row(BRANCH_W), row(BRANCH_W), row(BRANCH_W),
                 st(A_CONV - 1), st(D_CONV - 1)]
    return pl.pallas_call(
        functools.partial(_proj_kernel, tm=tm),
        grid=(B, nt), in_specs=in_specs, out_specs=out_specs, out_shape=out_shape,
        scratch_shapes=[pltpu.VMEM((tm + 8, BRANCH_W), F32), pltpu.VMEM((tm + 32, BRANCH_W), F32)],
        compiler_params=_params(2), name="proj",
    )(x, gpre, w_small, conva, qn, wuq, kvn, cdw, cdb, lng, lnb, bufa, bufd, cq, sq, ck, sk)


def _kvexpand_kernel(ckv_ref, kr_ref, wk_ref, e_ref, wvt_ref, k_ref, vt_ref):
    c = ckv_ref[0].astype(BF16)
    k_ref[0] = (_dot(c, wk_ref[...]) + _dot(kr_ref[0].astype(BF16), e_ref[...])).astype(BF16)
    vt_ref[0, 0] = lax.dot_general(wvt_ref[...], c, (((1,), (1,)), ((), ())),
                                   preferred_element_type=F32).astype(BF16)


def _kvexpand(ckv, kr, wk_pad, e_mat, wvt, tk):
    B, n, _ = ckv.shape
    nb = n // tk
    row = lambda w: pl.BlockSpec((1, tk, w), lambda b, j: (b, j, 0))
    return pl.pallas_call(
        _kvexpand_kernel, grid=(B, nb),
        in_specs=[row(KV_LORA), row(MLA_ROPE), _const_spec(wk_pad.shape), _const_spec(e_mat.shape),
                  _const_spec(wvt.shape)],
        out_specs=[row(MLA_HEADS * HEAD_PAD),
                   pl.BlockSpec((1, 1, MLA_HEADS * MLA_V, tk), lambda b, j: (b, j, 0, 0))],
        out_shape=[jax.ShapeDtypeStruct((B, n, MLA_HEADS * HEAD_PAD), BF16),
                   jax.ShapeDtypeStruct((B, nb, MLA_HEADS * MLA_V, tk), BF16)],
        compiler_params=_params(2), name="kvexpand",
    )(ckv, kr, wk_pad, e_mat, wvt)


def _attn_kernel(q_ref, k_ref, vt_ref, o_ref, qt_s, m_s, l_s, acc_s, *, causal, kv_valid, tq, tk):
    i = pl.program_id(1)
    qt_s[...] = q_ref[0].astype(F32).T.astype(BF16)
    m_s[...] = jnp.full(m_s.shape, -jnp.inf, F32)
    l_s[...] = jnp.zeros(l_s.shape, F32)
    acc_s[...] = jnp.zeros(acc_s.shape, F32)

    def block(j, mask):
        r0 = j * tk
        if not isinstance(r0, int):
            r0 = pl.multiple_of(r0, tk)
        for hh in range(MLA_HEADS):
            kh = k_ref[0, pl.ds(r0, tk), hh * HEAD_PAD:(hh + 1) * HEAD_PAD]
            s = _dot(kh, qt_s[hh * HEAD_PAD:(hh + 1) * HEAD_PAD, :])
            if mask is not None:
                s = jnp.where(mask, s, -jnp.inf)
            m_prev = m_s[hh:hh + 1, :]
            m_new = jnp.maximum(m_prev, jnp.max(s, axis=0, keepdims=True))
            alpha = jnp.exp2(m_prev - m_new)
            p = jnp.exp2(s - m_new)
            l_s[hh:hh + 1, :] = alpha * l_s[hh:hh + 1, :] + jnp.sum(p, axis=0, keepdims=True)
            vs = slice(hh * MLA_V, (hh + 1) * MLA_V)
            acc_s[vs, :] = alpha * acc_s[vs, :] + _dot(vt_ref[0, j, vs, :], p.astype(BF16))
            m_s[hh:hh + 1, :] = m_new

    if causal:
        ratio = tq // tk

        def full_block(j, carry):
            block(j, None)
            return carry

        lax.fori_loop(0, i * ratio, full_block, 0)
        qc = lax.broadcasted_iota(jnp.int32, (tk, tq), 1) // CHUNK
        for r in range(ratio):
            kc = (r * tk + lax.broadcasted_iota(jnp.int32, (tk, tq), 0)) // CHUNK
            block(i * ratio + r, kc <= qc)
    else:
        nkv = k_ref.shape[1] // tk
        for j in range(nkv):
            if (j + 1) * tk > kv_valid:
                block(j, j * tk + lax.broadcasted_iota(jnp.int32, (tk, tq), 0) < kv_valid)
            else:
                block(j, None)

    for hh in range(MLA_HEADS):
        vs = slice(hh * MLA_V, (hh + 1) * MLA_V)
        acc_s[vs, :] = acc_s[vs, :] / l_s[hh:hh + 1, :]
    o_ref[0] = acc_s[...].T.astype(BF16)


def _attn(q, k, vt, causal, kv_valid, tq, tk):
    B, T, _ = q.shape
    nk = k.shape[1]
    return pl.pallas_call(
        functools.partial(_attn_kernel, causal=causal, kv_valid=kv_valid, tq=tq, tk=tk),
        grid=(B, T // tq),
        in_specs=[pl.BlockSpec((1, tq, MLA_HEADS * HEAD_PAD), lambda b, i: (b, i, 0)),
                  pl.BlockSpec((1, nk, MLA_HEADS * HEAD_PAD), lambda b, i: (b, 0, 0),
                               pipeline_mode=pl.Buffered(1)),
                  pl.BlockSpec((1, nk // tk, MLA_HEADS * MLA_V, tk), lambda b, i: (b, 0, 0, 0),
                               pipeline_mode=pl.Buffered(1))],
        out_specs=pl.BlockSpec((1, tq, MLA_HEADS * MLA_V), lambda b, i: (b, i, 0)),
        out_shape=jax.ShapeDtypeStruct((B, T, MLA_HEADS * MLA_V), BF16),
        scratch_shapes=[pltpu.VMEM((MLA_HEADS * HEAD_PAD, tq), BF16), pltpu.VMEM((MLA_HEADS, tq), F32),
                        pltpu.VMEM((MLA_HEADS, tq), F32), pltpu.VMEM((MLA_HEADS * MLA_V, tq), F32)],
        compiler_params=_params(2), name="attn",
    )(q, k, vt)


def _hgrn_kernel(hq_ref, hf_ref, hi_ref, hg_ref, lbraw_ref, gn_ref, s0_ref,
                 o_ref, sfin_ref, st_s, *, layer, tm, cc):
    t = pl.program_id(1)
    nt = pl.num_programs(1)
    W = HG_HEADS * HG_D

    @pl.when(t == 0)
    def _():
        for hh in range(HG_HEADS):
            st_s[hh] = s0_ref[0, hh].T

    raw = lbraw_ref[...]
    e = jnp.exp(raw - jnp.max(raw, axis=0, keepdims=True))
    pr = e / jnp.sum(e, axis=0, keepdims=True)
    cum = pr[0:1, :]
    for r in range(1, layer + 1):
        cum = cum + pr[r:r + 1, :]
    lb = cum - pr[0:1, :]
    log_lb = jnp.log(lb)
    log_1mlb = jnp.log1p(-lb)
    gn = gn_ref[...]

    tri = (lax.broadcasted_iota(jnp.int32, (cc, cc), 0)
           >= lax.broadcasted_iota(jnp.int32, (cc, cc), 1)).astype(BF16)
    row_c = lax.broadcasted_iota(jnp.int32, (cc, 1), 0)
    lane_c = lax.broadcasted_iota(jnp.int32, (SUB, cc), 1)
    row_s = lax.broadcasted_iota(jnp.int32, (SUB, 1), 0)
    nsub = cc // SUB

    def chunk(c, carry):
        r0 = pl.multiple_of(c * cc, cc)
        zf = hf_ref[0, pl.ds(r0, cc), :]
        q = hq_ref[0, pl.ds(r0, cc), :].astype(F32)
        v = hi_ref[0, pl.ds(r0, cc), :]
        ls = jnp.minimum(zf, 0.0) - jnp.log1p(jnp.exp(-jnp.abs(zf)))
        bb = log_1mlb + ls
        g = jnp.maximum(log_lb, bb) + jnp.log1p(jnp.exp(-jnp.abs(log_lb - bb)))
        kk = (1.0 - lb) * _sigmoid(-zf)
        g1 = g.astype(BF16)
        r1 = g - g1.astype(F32)
        g2 = r1.astype(BF16)
        g3 = (r1 - g2.astype(F32)).astype(BF16)
        b = _dot(tri, g1) + _dot(tri, g2) + _dot(tri, g3)
        bl = b[cc - 1:cc, :]
        qb = (q * jnp.exp(b)).astype(BF16)
        kdec = (kk * jnp.exp(bl - b)).astype(BF16)
        ebl = jnp.exp(bl)

        a_rows = []
        for si in range(nsub):
            lo = si * SUB
            q_i = q[lo:lo + SUB, :]
            b_i = b[lo:lo + SUB, :]
            if si > 0:
                ref = b[lo - 1:lo, :]
                qt = (q_i * jnp.exp(b_i - ref)).astype(BF16)
                kt = (kk * jnp.exp(jnp.where(row_c < lo, ref - b, -jnp.inf))).astype(BF16)
            a_h = []
            for hh in range(HG_HEADS):
                sl = slice(hh * HG_D, (hh + 1) * HG_D)
                if si > 0:
                    a = lax.dot_general(qt[:, sl], kt[:, sl], (((1,), (1,)), ((), ())),
                                        preferred_element_type=F32)
                else:
                    a = jnp.zeros((SUB, cc), F32)
                a_h.append(a)
            for s in range(SUB):
                b_s = b[lo + s:lo + s + 1, :]
                k_s = kk[lo + s:lo + s + 1, :]
                z = q_i * k_s * jnp.exp(jnp.where(row_s >= s, b_i - b_s, -jnp.inf))
                for hh in range(HG_HEADS):
                    col = jnp.sum(z[:, hh * HG_D:(hh + 1) * HG_D], axis=-1, keepdims=True)
                    a_h[hh] = jnp.where(lane_c == lo + s, col, a_h[hh])
            a_rows.append(a_h)

        outs = []
        for hh in range(HG_HEADS):
            sl = slice(hh * HG_D, (hh + 1) * HG_D)
            a_full = jnp.concatenate([a_rows[si][hh] for si in range(nsub)], axis=0).astype(BF16)
            st = st_s[hh]
            o = lax.dot_general(qb[:, sl], st.astype(BF16), (((1,), (1,)), ((), ())),
                                preferred_element_type=F32)
            o = o + _dot(a_full, v[:, sl])
            st_s[hh] = st * ebl[:, sl] + lax.dot_general(v[:, sl], kdec[:, sl], (((0,), (0,)), ((), ())),
                                                         preferred_element_type=F32)
            outs.append(o * lax.rsqrt(jnp.mean(o * o, axis=-1, keepdims=True) + EPS))
        on = jnp.concatenate(outs, axis=-1) * gn
        o_ref[0, pl.ds(r0, cc), :] = (on * _silu(hg_ref[0, pl.ds(r0, cc), :].astype(F32))).astype(BF16)
        return carry

    lax.fori_loop(0, tm // cc, chunk, 0)

    @pl.when(t == nt - 1)
    def _():
        for hh in range(HG_HEADS):
            sfin_ref[0, hh] = st_s[hh].T


def _hgrn(hq, hf, hi, hg, lb_raw, gn, s0, layer, tm, cc):
    B, T, W = hq.shape
    row = pl.BlockSpec((1, tm, W), lambda b, t: (b, t, 0))
    st = pl.BlockSpec((1, HG_HEADS, HG_D, HG_D), lambda b, t: (b, 0, 0, 0))
    return pl.pallas_call(
        functools.partial(_hgrn_kernel, layer=layer, tm=tm, cc=cc),
        grid=(B, T // tm),
        in_specs=[row, row, row, row, _const_spec(lb_raw.shape), _const_spec((1, W)), st],
        out_specs=[row, st],
        out_shape=[jax.ShapeDtypeStruct((B, T, W), BF16),
                   jax.ShapeDtypeStruct((B, HG_HEADS, HG_D, HG_D), F32)],
        scratch_shapes=[pltpu.VMEM((HG_HEADS, HG_D, HG_D), F32)],
        compiler_params=_params(2), name="hgrn",
    )(hq, hf, hi, hg, lb_raw, gn, s0)


def _merge_kernel(x_ref, a_ref, b_ref, c_ref, d_ref, gpre_ref, wg_ref, bg_ref, wb_ref, wo_ref, gpost_ref,
                  y_ref):
    x = x_ref[...]
    h = _rms(x, gpre_ref[...]).astype(BF16)
    merged = None
    for i, br in enumerate((a_ref, b_ref, c_ref, d_ref)):
        gate = _sigmoid(_dot(h, wg_ref[:, i * D_MODEL:(i + 1) * D_MODEL])
                        + bg_ref[:, i * D_MODEL:(i + 1) * D_MODEL])
        term = gate * _dot(br[...], wb_ref[i])
        merged = term if merged is None else merged + term
    y = _dot(merged.astype(BF16), wo_ref[...])
    y_ref[...] = x + _rms(y, gpost_ref[...])


def _merge(x, br, gpre, wg, bg, wb, wo, gpost, tm):
    n = x.shape[0]
    row = lambda w: pl.BlockSpec((tm, w), lambda i: (i, 0))
    return pl.pallas_call(
        _merge_kernel, grid=(n // tm,),
        in_specs=[row(D_MODEL)] + [row(BRANCH_W)] * 4
                 + [_const_spec((1, D_MODEL)), _const_spec(wg.shape), _const_spec(bg.shape),
                    _const_spec(wb.shape), _const_spec(wo.shape), _const_spec((1, D_MODEL))],
        out_specs=row(D_MODEL),
        out_shape=jax.ShapeDtypeStruct((n, D_MODEL), F32),
        compiler_params=_params(1), name="merge",
    )(x, *br, gpre, wg, bg, wb, wo, gpost)


def _ffn_kernel(x_ref, gpre_ref, wup_ref, cw_ref, wdn_ref, gpost_ref, buf_ref,
                y_ref, nbuf_ref, g_s, *, tm, nsplit):
    t = pl.program_id(1)

    @pl.when(t == 0)
    def _():
        g_s[0:8, :] = jnp.zeros((8, D_FF), F32)
        g_s[8 - (FFN_CONV - 1):8, :] = buf_ref[0]

    x = x_ref[0]
    h = _rms(x, gpre_ref[...]).astype(BF16)
    wc = D_FF // nsplit
    ff = None
    for s in range(nsplit):
        lo = s * wc
        g = _dot(h, wup_ref[:, lo:lo + wc])
        g_s[8:8 + tm, lo:lo + wc] = g
        fg = (cw_ref[0:1, lo:lo + wc] * g_s[6:6 + tm, lo:lo + wc]
              + cw_ref[1:2, lo:lo + wc] * g_s[7:7 + tm, lo:lo + wc]
              + cw_ref[2:3, lo:lo + wc] * g)
        up = _dot(h, wup_ref[:, D_FF + lo:D_FF + lo + wc])
        part = _dot((_silu(fg) * up).astype(BF16), wdn_ref[lo:lo + wc, :])
        ff = part if ff is None else ff + part
    y_ref[0] = x + _rms(ff, gpost_ref[...])
    nbuf_ref[0] = g_s[tm + 6:tm + 8, :]
    g_s[0:8, :] = g_s[tm:tm + 8, :]


def _ffn(x, gpre, wup, cw, wdn, gpost, buf, tm):
    B, T, _ = x.shape
    row = pl.BlockSpec((1, tm, D_MODEL), lambda b, t: (b, t, 0))
    st = pl.BlockSpec((1, FFN_CONV - 1, D_FF), lambda b, t: (b, 0, 0))
    return pl.pallas_call(
        functools.partial(_ffn_kernel, tm=tm, nsplit=2),
        grid=(B, T // tm),
        in_specs=[row, _const_spec((1, D_MODEL)), _const_spec(wup.shape), _const_spec(cw.shape),
                  _const_spec(wdn.shape), _const_spec((1, D_MODEL)), st],
        out_specs=[row, st],
        out_shape=[jax.ShapeDtypeStruct((B, T, D_MODEL), F32),
                   jax.ShapeDtypeStruct((B, FFN_CONV - 1, D_FF), F32)],
        scratch_shapes=[pltpu.VMEM((tm + 8, D_FF), F32)],
        compiler_params=_params(2), name="ffn",
    )(x, gpre, wup, cw, wdn, gpost, buf)


def _tile(n, pref):
    return pref if n % pref == 0 else n


def _rope_tables(pos):
    half = MLA_ROPE // 2
    inv = ROPE_BASE ** (-jnp.arange(half, dtype=F32) / half)
    ang = pos.astype(F32)[:, None] * inv[None, :]
    cos, sin = jnp.cos(ang), jnp.sin(ang)
    n = pos.shape[0]
    scale = (MLA_NOPE + MLA_ROPE) ** -0.5 * np.log2(np.e)
    pad = jnp.zeros((n, HEAD_PAD - MLA_NOPE - MLA_ROPE), F32)
    cq = scale * jnp.concatenate([jnp.ones((n, MLA_NOPE), F32), cos, cos, pad], axis=1)
    sq = scale * jnp.concatenate([jnp.zeros((n, MLA_NOPE), F32), -sin, sin, pad], axis=1)
    ck = jnp.concatenate([cos, cos], axis=1)
    sk = jnp.concatenate([-sin, sin], axis=1)
    return cq, sq, ck, sk


def _layer_weights(l, norm_mix_pre, norm_mix_post, norm_ffn_pre, norm_ffn_post, w_in, b_gate, conv_a_w,
                   mla_q_norm, mla_w_uq, mla_kv_norm, mla_w_uk, mla_w_uv, hgrn_norm,
                   conv_d_w, conv_d_b, ln_d_g, ln_d_b, w_branch, w_out, ffn_w_up, ffn_conv_w, ffn_w_down):
    wi = w_in[l]
    o_kr = 3 * BRANCH_W + Q_LORA + KV_LORA
    o_h = o_kr + MLA_ROPE
    o_g = o_h + 6 * BRANCH_W
    half = MLA_ROPE // 2
    w_small = jnp.concatenate(
        [wi[:, :o_kr], wi[:, o_h:o_g], wi[:, o_kr:o_h], wi[:, o_kr + half:o_h], wi[:, o_kr:o_kr + half]],
        axis=1).astype(BF16)
    w_gate = wi[:, o_g:].astype(BF16)
    uq = mla_w_uq[l].reshape(Q_LORA, MLA_HEADS, MLA_NOPE + MLA_ROPE)
    nope, r1, r2 = uq[..., :MLA_NOPE], uq[..., MLA_NOPE:MLA_NOPE + half], uq[..., MLA_NOPE + half:]
    z_pad = jnp.zeros((Q_LORA, MLA_HEADS, HEAD_PAD - MLA_NOPE - MLA_ROPE), F32)
    z_nope = jnp.zeros((Q_LORA, MLA_HEADS, MLA_NOPE), F32)
    wuq = jnp.concatenate(
        [jnp.concatenate([nope, r1, r2, z_pad], axis=-1).reshape(Q_LORA, MLA_HEADS * HEAD_PAD),
         jnp.concatenate([z_nope, r2, r1, z_pad], axis=-1).reshape(Q_LORA, MLA_HEADS * HEAD_PAD)],
        axis=1).astype(BF16)
    uk = mla_w_uk[l].reshape(KV_LORA, MLA_HEADS, MLA_NOPE)
    wk_pad = jnp.concatenate([uk, jnp.zeros((KV_LORA, MLA_HEADS, HEAD_PAD - MLA_NOPE), F32)],
                             axis=-1).reshape(KV_LORA, MLA_HEADS * HEAD_PAD).astype(BF16)
    e_np = np.zeros((MLA_ROPE, MLA_HEADS, HEAD_PAD), np.float32)
    for r in range(MLA_ROPE):
        e_np[r, :, MLA_NOPE + r] = 1.0
    e_mat = jnp.asarray(e_np.reshape(MLA_ROPE, MLA_HEADS * HEAD_PAD), BF16)
    r2d = lambda a: a[l].reshape(1, -1)
    return dict(
        gpre=r2d(norm_mix_pre), gpost=r2d(norm_mix_post), fpre=r2d(norm_ffn_pre), fpost=r2d(norm_ffn_post),
        w_small=w_small, w_gate=w_gate, b_gate=r2d(b_gate), conva=conv_a_w[l], qn=r2d(mla_q_norm), wuq=wuq,
        kvn=r2d(mla_kv_norm), wk_pad=wk_pad, e_mat=e_mat, wvt=mla_w_uv[l].T.astype(BF16), gn=r2d(hgrn_norm),
        cdw=conv_d_w[l], cdb=r2d(conv_d_b), lng=r2d(ln_d_g), lnb=r2d(ln_d_b),
        wb=w_branch[l].astype(BF16), wo=w_out[l].astype(BF16), wup=ffn_w_up[l].astype(BF16),
        cw=ffn_conv_w[l], wdn=ffn_w_down[l].astype(BF16))


def _layer(x, tabs, past, state, w, lb_raw, layer):
    B, T, _ = x.shape
    buf_a, s0, buf_d, buf_f = state
    (out_a, out_d, q, ckv, kr, hq, hf, hi, hg, nbuf_a, nbuf_d) = _proj(
        x, w["gpre"], w["w_small"], w["conva"], w["qn"], w["wuq"], w["kvn"], w["cdw"], w["cdb"],
        w["lng"], w["lnb"], buf_a, buf_d, tabs, _tile(T, TILE_PROJ))

    if past is None:
        ckv_all, kr_all, kv_valid = ckv, kr, T
    else:
        n_keys = past[0].shape[1] + T
        n_pad = -n_keys % 128
        ckv_all = jnp.concatenate([past[0], ckv, jnp.zeros((B, n_pad, KV_LORA), F32)], axis=1)
        kr_all = jnp.concatenate([past[1], kr, jnp.zeros((B, n_pad, MLA_ROPE), F32)], axis=1)
        kv_valid = n_keys
    nk = ckv_all.shape[1]
    if past is None:
        tk = _tile(T, TILE_ATTN_K)
        tq = _tile(T, TILE_ATTN_Q)
        kcat, vt = _kvexpand(ckv_all, kr_all, w["wk_pad"], w["e_mat"], w["wvt"], tk)
        out_b = _attn(q, kcat, vt, True, kv_valid, tq, tk)
    else:
        kcat, vt = _kvexpand(ckv_all, kr_all, w["wk_pad"], w["e_mat"], w["wvt"], nk)
        q_pad = -T % 128
        q_in = jnp.concatenate([q, jnp.zeros((B, q_pad, q.shape[-1]), q.dtype)], axis=1)
        out_b = _attn(q_in, kcat, vt, False, kv_valid, T + q_pad, nk)[:, :T]

    cc = min(CHUNK, T)
    out_c, s_new = _hgrn(hq, hf, hi, hg, lb_raw, w["gn"], s0, layer, _tile(T, TILE_HGRN), cc)

    flat = lambda a: a.reshape(B * T, a.shape[-1])
    x2 = _merge(flat(x), [flat(out_a), flat(out_b), flat(out_c), flat(out_d)], w["gpre"], w["w_gate"],
                w["b_gate"], w["wb"], w["wo"], w["gpost"], _tile(B * T, TILE_MERGE))
    x3, nbuf_f = _ffn(x2.reshape(B, T, D_MODEL), w["fpre"], w["wup"], w["cw"], w["wdn"], w["fpost"], buf_f,
                      _tile(T, TILE_FFN))
    return x3, (ckv, kr, nbuf_a, s_new, nbuf_d, nbuf_f)


def kernel(x_prompt, x_sample, cache_ckv, cache_krope, state_conv_a, state_hgrn, state_conv_d, state_ffn_conv,
           norm_mix_pre, norm_mix_post, norm_ffn_pre, norm_ffn_post, w_in, b_gate, conv_a_w,
           mla_q_norm, mla_w_uq, mla_kv_norm, mla_w_uk, mla_w_uv, hgrn_lb_raw, hgrn_norm,
           conv_d_w, conv_d_b, ln_d_g, ln_d_b, w_branch, w_out, ffn_w_up, ffn_conv_w, ffn_w_down):
    depth = w_in.shape[0]
    B, T, _ = x_prompt.shape
    Bs, Ts, _ = x_sample.shape
    past_len = cache_ckv.shape[2]
    tabs_p = _rope_tables(jnp.arange(T))
    tabs_s = _rope_tables(past_len + jnp.arange(Ts))
    zero_state = (jnp.zeros((B, A_CONV - 1, BRANCH_W), F32), jnp.zeros((B, HG_HEADS, HG_D, HG_D), F32),
                  jnp.zeros((B, D_CONV - 1, BRANCH_W), F32), jnp.zeros((B, FFN_CONV - 1, D_FF), F32))
    lb_raw = hgrn_lb_raw.astype(F32)
    yp, ys = x_prompt, x_sample
    p_states, s_states = [], []
    for l in range(depth):
        w = _layer_weights(l, norm_mix_pre, norm_mix_post, norm_ffn_pre, norm_ffn_post, w_in, b_gate, conv_a_w,
                           mla_q_norm, mla_w_uq, mla_kv_norm, mla_w_uk, mla_w_uv, hgrn_norm,
                           conv_d_w, conv_d_b, ln_d_g, ln_d_b, w_branch, w_out, ffn_w_up, ffn_conv_w,
                           ffn_w_down)
        yp, sp = _layer(yp, tabs_p, None, zero_state, w, lb_raw, l)
        ys, ss = _layer(ys, tabs_s, (cache_ckv[l], cache_krope[l]),
                        (state_conv_a[l], state_hgrn[l], state_conv_d[l], state_ffn_conv[l]), w, lb_raw, l)
        p_states.append(sp)
        s_states.append(ss)
    ps = [jnp.stack(z) for z in zip(*p_states)]
    sst = [jnp.stack(z) for z in zip(*s_states)]
    return (yp, ys, ps[0], ps[1], ps[2], ps[3], ps[4], ps[5], sst[0], sst[1], sst[2], sst[3], sst[4], sst[5])
```

```python
import functools

import numpy as np
import jax
import jax.numpy as jnp
from jax import lax
from jax.experimental import pallas as pl
from jax.experimental.pallas import tpu as pltpu

D_MODEL = 1024
BRANCH_W = 512
N_BRANCH = 4
MLA_HEADS = 8
MLA_NOPE = 64
MLA_ROPE = 32
MLA_V = 64
Q_LORA = 384
KV_LORA = 256
ROPE_BASE = 10000.0
HG_HEADS = 4
HG_D = 128
A_CONV = 3
D_CONV = 31
FFN_CONV = 3
D_FF = 2816
CHUNK = 64
EPS = 1e-6
LOG2E = 1.4426950408889634
HEAD_PAD = 128
SUB = 8

_C_A = 0
_C_CQ = 3 * BRANCH_W
_C_CKV = _C_CQ + Q_LORA
_C_HG = _C_CKV + KV_LORA
_C_DIN = _C_HG + 4 * BRANCH_W
_C_KR = _C_DIN + 2 * BRANCH_W
_C_END = _C_KR + 2 * MLA_ROPE

TILE_PROJ = 512
TILE_ATTN_Q = 1024
TILE_ATTN_K = 512
ATTN_STRIP = 256
ATTN_LOOKAHEAD = 3
L_ROWS = 16
TILE_HGRN = 512
TILE_MERGE = 512
TILE_FFN = 512
VMEM_LIMIT = 56 * 1024 * 1024

F32 = jnp.float32
BF16 = jnp.bfloat16


def _rms(x, g):
    return x * lax.rsqrt(jnp.mean(x * x, axis=-1, keepdims=True) + EPS) * g


def _sigmoid(x):
    return 1.0 / (1.0 + jnp.exp(-x))


def _silu(x):
    return x * _sigmoid(x)


def _dot(a, b):
    return jnp.dot(a, b, preferred_element_type=F32)


def _const_spec(shape):
    nd = len(shape)
    return pl.BlockSpec(shape, lambda *_: (0,) * nd, pipeline_mode=pl.Buffered(1))


def _params(n_axes):
    return pltpu.CompilerParams(dimension_semantics=("arbitrary",) * n_axes,
                                vmem_limit_bytes=VMEM_LIMIT)


def _proj_kernel(x_ref, gpre_ref, w_ref, conva_ref, qn_ref, wuq_ref, kvn_ref,
                 cdw_ref, cdb_ref, lng_ref, lnb_ref, bufa_ref, bufd_ref,
                 cq_ref, sq_ref, ck_ref, sk_ref,
                 outa_ref, outd_ref, q_ref, ckv_ref, kr_ref,
                 hq_ref, hf_ref, hi_ref, hg_ref, nbufa_ref, nbufd_ref,
                 ua_s, gd_s, *, tm):
    t = pl.program_id(1)

    @pl.when(t == 0)
    def _():
        ua_s[0:8, :] = jnp.zeros((8, BRANCH_W), F32)
        ua_s[8 - (A_CONV - 1):8, :] = bufa_ref[0]
        gd_s[0:32, :] = jnp.zeros((32, BRANCH_W), F32)
        gd_s[32 - (D_CONV - 1):32, :] = bufd_ref[0]
        gd_s[tm + 32:tm + 40, :] = jnp.zeros((8, BRANCH_W), F32)

    h = _rms(x_ref[0], gpre_ref[...]).astype(BF16)

    pd = _dot(h, w_ref[:, _C_DIN:_C_KR])
    gd_s[32:32 + tm, :] = pd[:, :BRANCH_W] * _sigmoid(pd[:, BRANCH_W:])

    base = 32 - (D_CONV - 1)
    yd = [cdb_ref[...]]

    def conv_group(r):
        ks = [k for k in range(D_CONV) if (base + k) % 8 == r]
        part = None
        for k in ks:
            lo = base + k - r
            term = cdw_ref[k:k + 1, :] * gd_s[lo:lo + tm + 8, :]
            part = term if part is None else part + term
        yd[0] = yd[0] + part[r:r + tm, :]

    a_b = _dot(h, w_ref[:, _C_A:_C_A + BRANCH_W])
    conv_group(0)
    a_c = _dot(h, w_ref[:, _C_A + BRANCH_W:_C_A + 2 * BRANCH_W])
    conv_group(1)
    a_x = _dot(h, w_ref[:, _C_A + 2 * BRANCH_W:_C_CQ])
    u = a_c * a_x
    ua_s[8:8 + tm, :] = u
    ya = (conva_ref[0:1, :] * ua_s[6:6 + tm, :] + conva_ref[1:2, :] * ua_s[7:7 + tm, :]
          + conva_ref[2:3, :] * u)
    outa_ref[0] = (a_b * ya).astype(BF16)
    nbufa_ref[0] = ua_s[tm + 6:tm + 8, :]
    ua_s[0:8, :] = ua_s[tm:tm + 8, :]
    conv_group(2)

    pq = _dot(h, w_ref[:, _C_CQ:_C_HG])
    cqn = _rms(pq[:, :Q_LORA], qn_ref[...]).astype(BF16)
    ckv_ref[0] = _rms(pq[:, Q_LORA:], kvn_ref[...])
    conv_group(3)
    qab = _dot(cqn, wuq_ref[...])
    cq = cq_ref[...]
    sq = sq_ref[...]
    nq = MLA_HEADS * HEAD_PAD
    for hh in range(MLA_HEADS):
        lo = hh * HEAD_PAD
        q_ref[0, :, lo:lo + HEAD_PAD] = (qab[:, lo:lo + HEAD_PAD] * cq
                                         + qab[:, nq + lo:nq + lo + HEAD_PAD] * sq).astype(BF16)
    conv_group(4)
    pk = _dot(h, w_ref[:, _C_KR:_C_END])
    kr_ref[0] = pk[:, :MLA_ROPE] * ck_ref[...] + pk[:, MLA_ROPE:] * sk_ref[...]

    hq_ref[0] = _dot(h, w_ref[:, _C_HG:_C_HG + BRANCH_W]).astype(BF16)
    conv_group(5)
    hf_ref[0] = _dot(h, w_ref[:, _C_HG + BRANCH_W:_C_HG + 2 * BRANCH_W])
    conv_group(6)
    hi_ref[0] = _dot(h, w_ref[:, _C_HG + 2 * BRANCH_W:_C_HG + 3 * BRANCH_W]).astype(BF16)
    conv_group(7)
    hg_ref[0] = _dot(h, w_ref[:, _C_HG + 3 * BRANCH_W:_C_DIN]).astype(BF16)

    yd = yd[0]
    mu = jnp.mean(yd, axis=-1, keepdims=True)
    yc = yd - mu
    ln = yc * lax.rsqrt(jnp.mean(yc * yc, axis=-1, keepdims=True) + EPS) * lng_ref[...] + lnb_ref[...]
    outd_ref[0] = _silu(ln).astype(BF16)
    nbufd_ref[0] = gd_s[tm + 2:tm + 32, :]
    gd_s[0:32, :] = gd_s[tm:tm + 32, :]


def _proj(x, gpre, w_small, conva, qn, wuq, kvn, cdw, cdb, lng, lnb, bufa, bufd, tabs, tm):
    B, T, _ = x.shape
    cq, sq, ck, sk = tabs
    nt = T // tm
    row = lambda w: pl.BlockSpec((1, tm, w), lambda b, t: (b, t, 0))
    tab = lambda w: pl.BlockSpec((tm, w), lambda b, t: (t, 0))
    st = lambda r: pl.BlockSpec((1, r, BRANCH_W), lambda b, t: (b, 0, 0))
    in_specs = [row(D_MODEL), _const_spec((1, D_MODEL)), _const_spec(w_small.shape),
                _const_spec(conva.shape), _const_spec((1, Q_LORA)), _const_spec(wuq.shape),
                _const_spec((1, KV_LORA)), _const_spec(cdw.shape), _const_spec((1, BRANCH_W)),
                _const_spec((1, BRANCH_W)), _const_spec((1, BRANCH_W)),
                st(A_CONV - 1), st(D_CONV - 1),
                tab(HEAD_PAD), tab(HEAD_PAD), tab(MLA_ROPE), tab(MLA_ROPE)]
    out_shape = [jax.ShapeDtypeStruct((B, T, BRANCH_W), BF16),
                 jax.ShapeDtypeStruct((B, T, BRANCH_W), BF16),
                 jax.ShapeDtypeStruct((B, T, MLA_HEADS * HEAD_PAD), BF16),
                 jax.ShapeDtypeStruct((B, T, KV_LORA), F32),
                 jax.ShapeDtypeStruct((B, T, MLA_ROPE), F32),
                 jax.ShapeDtypeStruct((B, T, BRANCH_W), BF16),
                 jax.ShapeDtypeStruct((B, T, BRANCH_W), F32),
                 jax.ShapeDtypeStruct((B, T, BRANCH_W), BF16),
                 jax.ShapeDtypeStruct((B, T, BRANCH_W), BF16),
                 jax.ShapeDtypeStruct((B, A_CONV - 1, BRANCH_W), F32),
                 jax.ShapeDtypeStruct((B, D_CONV - 1, BRANCH_W), F32)]
    out_specs = [row(BRANCH_W), row(BRANCH_W), row(MLA_HEADS * HEAD_PAD), row(KV_LORA), row(MLA_ROPE),
                 row(BRANCH_W), row(BRANCH_W), row(BRANCH_W), row(BRANCH_W),
                 st(A_CONV - 1), st(D_CONV - 1)]
    return pl.pallas_call(
        functools.partial(_proj_kernel, tm=tm),
        grid=(B, nt), in_specs=in_specs, out_specs=out_specs, out_shape=out_shape,
        scratch_shapes=[pltpu.VMEM((tm + 8, BRANCH_W), F32), pltpu.VMEM((tm + 40, BRANCH_W), F32)],
        compiler_params=_params(2), name="proj",
    )(x, gpre, w_small, conva, qn, wuq, kvn, cdw, cdb, lng, lnb, bufa, bufd, cq, sq, ck, sk)


def _kvexpand_kernel(ckv_ref, kr_ref, wk_ref, e_ref, wvt_ref, k_ref, vt_ref):
    c = ckv_ref[0].astype(BF16)
    k_ref[0] = (_dot(c, wk_ref[...]) + _dot(kr_ref[0].astype(BF16), e_ref[...])).astype(BF16)
    vt_ref[0, 0] = lax.dot_general(wvt_ref[...], c, (((1,), (1,)), ((), ())),
                                   preferred_element_type=F32).astype(BF16)


def _kvexpand(ckv, kr, wk_pad, e_mat, wvt, tk):
    B, n, _ = ckv.shape
    nb = n // tk
    row = lambda w: pl.BlockSpec((1, tk, w), lambda b, j: (b, j, 0))
    return pl.pallas_call(
        _kvexpand_kernel, grid=(B, nb),
        in_specs=[row(KV_LORA), row(MLA_ROPE), _const_spec(wk_pad.shape), _const_spec(e_mat.shape),
                  _const_spec(wvt.shape)],
        out_specs=[row(MLA_HEADS * HEAD_PAD),
                   pl.BlockSpec((1, 1, MLA_HEADS * MLA_V, tk), lambda b, j: (b, j, 0, 0))],
        out_shape=[jax.ShapeDtypeStruct((B, n, MLA_HEADS * HEAD_PAD), BF16),
                   jax.ShapeDtypeStruct((B, nb, MLA_HEADS * MLA_V, tk), BF16)],
        compiler_params=_params(2), name="kvexpand",
    )(ckv, kr, wk_pad, e_mat, wvt)


def _attn_kernel(q_ref, k_ref, vt_ref, o_ref, qt_s, m_s, acc_s, *, causal, kv_valid, tq, tk):
    i = pl.program_id(1)
    strip = min(ATTN_STRIP, tq)
    qt_s[...] = q_ref[0].astype(F32).T.astype(BF16)
    m_s[...] = jnp.full(m_s.shape, -jnp.inf, F32)
    acc_s[...] = jnp.zeros(acc_s.shape, F32)
    ones_rows = jnp.ones((L_ROWS, tk), BF16)
    units = [(hh, c) for hh in range(MLA_HEADS) for c in range(tq // strip)]

    def block(j, mask):
        r0 = j * tk
        if not isinstance(r0, int):
            r0 = pl.multiple_of(r0, tk)

        def scores(u):
            hh, c = u
            kh = k_ref[0, pl.ds(r0, tk), hh * HEAD_PAD:(hh + 1) * HEAD_PAD]
            return _dot(kh, qt_s[hh * HEAD_PAD:(hh + 1) * HEAD_PAD, c * strip:(c + 1) * strip])

        pend = [scores(u) for u in units[:ATTN_LOOKAHEAD]]
        for n, (hh, c) in enumerate(units):
            cs = slice(c * strip, (c + 1) * strip)
            s = pend.pop(0)
            if mask is not None:
                s = jnp.where(mask[:, cs], s, -jnp.inf)
            m_prev = m_s[hh:hh + 1, cs]
            m_new = jnp.maximum(m_prev, jnp.max(s, axis=0, keepdims=True))
            alpha = jnp.exp2(m_prev - m_new)
            p = jnp.exp2(s - m_new).astype(BF16)
            if n + ATTN_LOOKAHEAD < len(units):
                pend.append(scores(units[n + ATTN_LOOKAHEAD]))
            va = jnp.concatenate([vt_ref[0, j, hh * MLA_V:(hh + 1) * MLA_V, :], ones_rows], axis=0)
            acc_s[hh, :, cs] = alpha * acc_s[hh, :, cs] + _dot(va, p)
            m_s[hh:hh + 1, cs] = m_new

    if causal:
        ratio = tq // tk

        def full_block(j, carry):
            block(j, None)
            return carry

        lax.fori_loop(0, i * ratio, full_block, 0)
        qc = lax.broadcasted_iota(jnp.int32, (tk, tq), 1) // CHUNK
        for r in range(ratio):
            kc = (r * tk + lax.broadcasted_iota(jnp.int32, (tk, tq), 0)) // CHUNK
            block(i * ratio + r, kc <= qc)
    else:
        nkv = k_ref.shape[1] // tk
        for j in range(nkv):
            if (j + 1) * tk > kv_valid:
                block(j, j * tk + lax.broadcasted_iota(jnp.int32, (tk, tq), 0) < kv_valid)
            else:
                block(j, None)

    outs = [acc_s[hh, :MLA_V, :] / acc_s[hh, MLA_V:MLA_V + 1, :] for hh in range(MLA_HEADS)]
    o_ref[0] = jnp.concatenate(outs, axis=0).T.astype(BF16)


def _attn(q, k, vt, causal, kv_valid, tq, tk):
    B, T, _ = q.shape
    nk = k.shape[1]
    return pl.pallas_call(
        functools.partial(_attn_kernel, causal=causal, kv_valid=kv_valid, tq=tq, tk=tk),
        grid=(B, T // tq),
        in_specs=[pl.BlockSpec((1, tq, MLA_HEADS * HEAD_PAD), lambda b, i: (b, i, 0)),
                  pl.BlockSpec((1, nk, MLA_HEADS * HEAD_PAD), lambda b, i: (b, 0, 0),
                               pipeline_mode=pl.Buffered(1)),
                  pl.BlockSpec((1, nk // tk, MLA_HEADS * MLA_V, tk), lambda b, i: (b, 0, 0, 0),
                               pipeline_mode=pl.Buffered(1))],
        out_specs=pl.BlockSpec((1, tq, MLA_HEADS * MLA_V), lambda b, i: (b, i, 0)),
        out_shape=jax.ShapeDtypeStruct((B, T, MLA_HEADS * MLA_V), BF16),
        scratch_shapes=[pltpu.VMEM((MLA_HEADS * HEAD_PAD, tq), BF16), pltpu.VMEM((MLA_HEADS, tq), F32),
                        pltpu.VMEM((MLA_HEADS, MLA_V + L_ROWS, tq), F32)],
        compiler_params=_params(2), name="attn",
    )(q, k, vt)


def _hgrn_kernel(hq_ref, hf_ref, hi_ref, hg_ref, lbraw_ref, gn_ref, s0_ref,
                 o_ref, sfin_ref, st_s, b2_s, c2_s, *, layer, tm, cc):
    t = pl.program_id(1)
    nt = pl.num_programs(1)
    W = HG_HEADS * HG_D

    @pl.when(t == 0)
    def _():
        for hh in range(HG_HEADS):
            st_s[hh] = s0_ref[0, hh].T

    raw = lbraw_ref[...]
    e = jnp.exp(raw - jnp.max(raw, axis=0, keepdims=True))
    pr = e / jnp.sum(e, axis=0, keepdims=True)
    cum = pr[0:1, :]
    for r in range(1, layer + 1):
        cum = cum + pr[r:r + 1, :]
    lb = cum - pr[0:1, :]
    log_lb = jnp.log(lb)
    log_1mlb = jnp.log1p(-lb)
    gn = gn_ref[...]

    tri = (lax.broadcasted_iota(jnp.int32, (cc, cc), 0)
           >= lax.broadcasted_iota(jnp.int32, (cc, cc), 1)).astype(BF16)
    lane_c = lax.broadcasted_iota(jnp.int32, (SUB, cc), 1)
    row_s = lax.broadcasted_iota(jnp.int32, (SUB, 1), 0)
    nsub = cc // SUB

    for c in range(tm // cc):
        rows = slice(c * cc, (c + 1) * cc)
        zf = hf_ref[0, rows, :]
        ls = jnp.minimum(zf, 0.0) - jnp.log(1.0 + jnp.exp(-jnp.abs(zf)))
        bb = log_1mlb + ls
        g = jnp.maximum(log_lb, bb) + jnp.log(1.0 + jnp.exp(-jnp.abs(log_lb - bb)))
        g1 = g.astype(BF16)
        r1 = g - g1.astype(F32)
        g2 = r1.astype(BF16)
        g3 = (r1 - g2.astype(F32)).astype(BF16)
        b2 = (_dot(tri, g1) + _dot(tri, g2) + _dot(tri, g3)) * LOG2E
        b2_s[rows, :] = b2
        c2_s[rows, :] = b2 - (bb - zf) * LOG2E

    def chunk(c, carry):
        r0 = pl.multiple_of(c * cc, cc)
        q = hq_ref[0, pl.ds(r0, cc), :].astype(F32)
        v = hi_ref[0, pl.ds(r0, cc), :]
        b2 = b2_s[pl.ds(r0, cc), :]
        c2 = c2_s[pl.ds(r0, cc), :]
        bl2 = b2[cc - 1:cc, :]
        qb = (q * jnp.exp2(b2)).astype(BF16)
        kdec = jnp.exp2(bl2 - c2).astype(BF16)
        ebl = jnp.exp2(bl2)

        o_inter = []
        for hh in range(HG_HEADS):
            sl = slice(hh * HG_D, (hh + 1) * HG_D)
            st = st_s[hh]
            o_inter.append(lax.dot_general(qb[:, sl], st.astype(BF16), (((1,), (1,)), ((), ())),
                                           preferred_element_type=F32))
            st_s[hh] = st * ebl[:, sl] + lax.dot_general(v[:, sl], kdec[:, sl], (((0,), (0,)), ((), ())),
                                                         preferred_element_type=F32)

        a_rows = []
        for si in range(nsub):
            lo = si * SUB
            q_i = q[lo:lo + SUB, :]
            b_i = b2[lo:lo + SUB, :]
            if si > 0:
                ref = b2[lo - 1:lo, :]
                qt = (q_i * jnp.exp2(b_i - ref)).astype(BF16)
                kt = jnp.concatenate([jnp.exp2(ref - c2[:lo, :]),
                                      jnp.zeros((cc - lo, W), F32)], axis=0).astype(BF16)
            a_h = []
            for hh in range(HG_HEADS):
                sl = slice(hh * HG_D, (hh + 1) * HG_D)
                if si > 0:
                    a = lax.dot_general(qt[:, sl], kt[:, sl], (((1,), (1,)), ((), ())),
                                        preferred_element_type=F32)
                else:
                    a = jnp.zeros((SUB, cc), F32)
                a_h.append(a)
            for s in range(SUB):
                z = q_i * jnp.exp2(b_i - c2[lo + s:lo + s + 1, :])
                place = (lane_c == lo + s) & (row_s >= s)
                for hh in range(HG_HEADS):
                    col = jnp.sum(z[:, hh * HG_D:(hh + 1) * HG_D], axis=-1, keepdims=True)
                    a_h[hh] = jnp.where(place, col, a_h[hh])
            a_rows.append(a_h)

        outs = []
        for hh in range(HG_HEADS):
            sl = slice(hh * HG_D, (hh + 1) * HG_D)
            a_full = jnp.concatenate([a_rows[si][hh] for si in range(nsub)], axis=0).astype(BF16)
            o = o_inter[hh] + _dot(a_full, v[:, sl])
            outs.append(o * lax.rsqrt(jnp.mean(o * o, axis=-1, keepdims=True) + EPS))
        on = jnp.concatenate(outs, axis=-1) * gn
        o_ref[0, pl.ds(r0, cc), :] = (on * _silu(hg_ref[0, pl.ds(r0, cc), :].astype(F32))).astype(BF16)
        return carry

    lax.fori_loop(0, tm // cc, chunk, 0)

    @pl.when(t == nt - 1)
    def _():
        for hh in range(HG_HEADS):
            sfin_ref[0, hh] = st_s[hh].T


def _hgrn(hq, hf, hi, hg, lb_raw, gn, s0, layer, tm, cc):
    B, T, W = hq.shape
    row = pl.BlockSpec((1, tm, W), lambda b, t: (b, t, 0))
    st = pl.BlockSpec((1, HG_HEADS, HG_D, HG_D), lambda b, t: (b, 0, 0, 0))
    return pl.pallas_call(
        functools.partial(_hgrn_kernel, layer=layer, tm=tm, cc=cc),
        grid=(B, T // tm),
        in_specs=[row, row, row, row, _const_spec(lb_raw.shape), _const_spec((1, W)), st],
        out_specs=[row, st],
        out_shape=[jax.ShapeDtypeStruct((B, T, W), BF16),
                   jax.ShapeDtypeStruct((B, HG_HEADS, HG_D, HG_D), F32)],
        scratch_shapes=[pltpu.VMEM((HG_HEADS, HG_D, HG_D), F32), pltpu.VMEM((tm, W), F32),
                        pltpu.VMEM((tm, W), F32)],
        compiler_params=_params(2), name="hgrn",
    )(hq, hf, hi, hg, lb_raw, gn, s0)


def _merge_kernel(x_ref, a_ref, b_ref, c_ref, d_ref, gpre_ref, wg_ref, bg_ref, wb_ref, wo_ref, gpost_ref,
                  y_ref):
    x = x_ref[...]
    h = _rms(x, gpre_ref[...]).astype(BF16)
    merged = None
    for i, br in enumerate((a_ref, b_ref, c_ref, d_ref)):
        gate = _sigmoid(_dot(h, wg_ref[:, i * D_MODEL:(i + 1) * D_MODEL])
                        + bg_ref[:, i * D_MODEL:(i + 1) * D_MODEL])
        term = gate * _dot(br[...], wb_ref[i])
        merged = term if merged is None else merged + term
    y = _dot(merged.astype(BF16), wo_ref[...])
    y_ref[...] = x + _rms(y, gpost_ref[...])


def _merge(x, br, gpre, wg, bg, wb, wo, gpost, tm):
    n = x.shape[0]
    row = lambda w: pl.BlockSpec((tm, w), lambda i: (i, 0))
    return pl.pallas_call(
        _merge_kernel, grid=(n // tm,),
        in_specs=[row(D_MODEL)] + [row(BRANCH_W)] * 4
                 + [_const_spec((1, D_MODEL)), _const_spec(wg.shape), _const_spec(bg.shape),
                    _const_spec(wb.shape), _const_spec(wo.shape), _const_spec((1, D_MODEL))],
        out_specs=row(D_MODEL),
        out_shape=jax.ShapeDtypeStruct((n, D_MODEL), F32),
        compiler_params=_params(1), name="merge",
    )(x, *br, gpre, wg, bg, wb, wo, gpost)


def _ffn_kernel(x_ref, gpre_ref, wup_ref, cw_ref, wdn_ref, gpost_ref, buf_ref,
                y_ref, nbuf_ref, g_s, *, tm, nsplit):
    t = pl.program_id(1)

    @pl.when(t == 0)
    def _():
        g_s[0:8, :] = jnp.zeros((8, D_FF), F32)
        g_s[8 - (FFN_CONV - 1):8, :] = buf_ref[0]

    x = x_ref[0]
    h = _rms(x, gpre_ref[...]).astype(BF16)
    wc = D_FF // nsplit
    ff = None
    for s in range(nsplit):
        lo = s * wc
        g = _dot(h, wup_ref[:, lo:lo + wc])
        g_s[8:8 + tm, lo:lo + wc] = g
        fg = (cw_ref[0:1, lo:lo + wc] * g_s[6:6 + tm, lo:lo + wc]
              + cw_ref[1:2, lo:lo + wc] * g_s[7:7 + tm, lo:lo + wc]
              + cw_ref[2:3, lo:lo + wc] * g)
        up = _dot(h, wup_ref[:, D_FF + lo:D_FF + lo + wc])
        part = _dot((_silu(fg) * up).astype(BF16), wdn_ref[lo:lo + wc, :])
        ff = part if ff is None else ff + part
    y_ref[0] = x + _rms(ff, gpost_ref[...])
    nbuf_ref[0] = g_s[tm + 6:tm + 8, :]
    g_s[0:8, :] = g_s[tm:tm + 8, :]


def _ffn(x, gpre, wup, cw, wdn, gpost, buf, tm):
    B, T, _ = x.shape
    row = pl.BlockSpec((1, tm, D_MODEL), lambda b, t: (b, t, 0))
    st = pl.BlockSpec((1, FFN_CONV - 1, D_FF), lambda b, t: (b, 0, 0))
    return pl.pallas_call(
        functools.partial(_ffn_kernel, tm=tm, nsplit=2),
        grid=(B, T // tm),
        in_specs=[row, _const_spec((1, D_MODEL)), _const_spec(wup.shape), _const_spec(cw.shape),
                  _const_spec(wdn.shape), _const_spec((1, D_MODEL)), st],
        out_specs=[row, st],
        out_shape=[jax.ShapeDtypeStruct((B, T, D_MODEL), F32),
                   jax.ShapeDtypeStruct((B, FFN_CONV - 1, D_FF), F32)],
        scratch_shapes=[pltpu.VMEM((tm + 8, D_FF), F32)],
        compiler_params=_params(2), name="ffn",
    )(x, gpre, wup, cw, wdn, gpost, buf)


def _tile(n, pref):
    return pref if n % pref == 0 else n


def _rope_tables(pos):
    half = MLA_ROPE // 2
    inv = ROPE_BASE ** (-jnp.arange(half, dtype=F32) / half)
    ang = pos.astype(F32)[:, None] * inv[None, :]
    cos, sin = jnp.cos(ang), jnp.sin(ang)
    n = pos.shape[0]
    scale = (MLA_NOPE + MLA_ROPE) ** -0.5 * np.log2(np.e)
    pad = jnp.zeros((n, HEAD_PAD - MLA_NOPE - MLA_ROPE), F32)
    cq = scale * jnp.concatenate([jnp.ones((n, MLA_NOPE), F32), cos, cos, pad], axis=1)
    sq = scale * jnp.concatenate([jnp.zeros((n, MLA_NOPE), F32), -sin, sin, pad], axis=1)
    ck = jnp.concatenate([cos, cos], axis=1)
    sk = jnp.concatenate([-sin, sin], axis=1)
    return cq, sq, ck, sk


def _layer_weights(l, norm_mix_pre, norm_mix_post, norm_ffn_pre, norm_ffn_post, w_in, b_gate, conv_a_w,
                   mla_q_norm, mla_w_uq, mla_kv_norm, mla_w_uk, mla_w_uv, hgrn_norm,
                   conv_d_w, conv_d_b, ln_d_g, ln_d_b, w_branch, w_out, ffn_w_up, ffn_conv_w, ffn_w_down):
    wi = w_in[l]
    o_kr = 3 * BRANCH_W + Q_LORA + KV_LORA
    o_h = o_kr + MLA_ROPE
    o_g = o_h + 6 * BRANCH_W
    half = MLA_ROPE // 2
    w_small = jnp.concatenate(
        [wi[:, :o_kr], wi[:, o_h:o_g], wi[:, o_kr:o_h], wi[:, o_kr + half:o_h], wi[:, o_kr:o_kr + half]],
        axis=1).astype(BF16)
    w_gate = wi[:, o_g:].astype(BF16)
    uq = mla_w_uq[l].reshape(Q_LORA, MLA_HEADS, MLA_NOPE + MLA_ROPE)
    nope, r1, r2 = uq[..., :MLA_NOPE], uq[..., MLA_NOPE:MLA_NOPE + half], uq[..., MLA_NOPE + half:]
    z_pad = jnp.zeros((Q_LORA, MLA_HEADS, HEAD_PAD - MLA_NOPE - MLA_ROPE), F32)
    z_nope = jnp.zeros((Q_LORA, MLA_HEADS, MLA_NOPE), F32)
    wuq = jnp.concatenate(
        [jnp.concatenate([nope, r1, r2, z_pad], axis=-1).reshape(Q_LORA, MLA_HEADS * HEAD_PAD),
         jnp.concatenate([z_nope, r2, r1, z_pad], axis=-1).reshape(Q_LORA, MLA_HEADS * HEAD_PAD)],
        axis=1).astype(BF16)
    uk = mla_w_uk[l].reshape(KV_LORA, MLA_HEADS, MLA_NOPE)
    wk_pad = jnp.concatenate([uk, jnp.zeros((KV_LORA, MLA_HEADS, HEAD_PAD - MLA_NOPE), F32)],
                             axis=-1).reshape(KV_LORA, MLA_HEADS * HEAD_PAD).astype(BF16)
    e_np = np.zeros((MLA_ROPE, MLA_HEADS, HEAD_PAD), np.float32)
    for r in range(MLA_ROPE):
        e_np[r, :, MLA_NOPE + r] = 1.0
    e_mat = jnp.asarray(e_np.reshape(MLA_ROPE, MLA_HEADS * HEAD_PAD), BF16)
    r2d = lambda a: a[l].reshape(1, -1)
    return dict(
        gpre=r2d(norm_mix_pre), gpost=r2d(norm_mix_post), fpre=r2d(norm_ffn_pre), fpost=r2d(norm_ffn_post),
        w_small=w_small, w_gate=w_gate, b_gate=r2d(b_gate), conva=conv_a_w[l], qn=r2d(mla_q_norm), wuq=wuq,
        kvn=r2d(mla_kv_norm), wk_pad=wk_pad, e_mat=e_mat, wvt=mla_w_uv[l].T.astype(BF16), gn=r2d(hgrn_norm),
        cdw=conv_d_w[l], cdb=r2d(conv_d_b), lng=r2d(ln_d_g), lnb=r2d(ln_d_b),
        wb=w_branch[l].astype(BF16), wo=w_out[l].astype(BF16), wup=ffn_w_up[l].astype(BF16),
        cw=ffn_conv_w[l], wdn=ffn_w_down[l].astype(BF16))


def _layer(x, tabs, past, state, w, lb_raw, layer):
    B, T, _ = x.shape
    buf_a, s0, buf_d, buf_f = state
    (out_a, out_d, q, ckv, kr, hq, hf, hi, hg, nbuf_a, nbuf_d) = _proj(
        x, w["gpre"], w["w_small"], w["conva"], w["qn"], w["wuq"], w["kvn"], w["cdw"], w["cdb"],
        w["lng"], w["lnb"], buf_a, buf_d, tabs, _tile(T, TILE_PROJ))

    if past is None:
        ckv_all, kr_all, kv_valid = ckv, kr, T
    else:
        n_keys = past[0].shape[1] + T
        n_pad = -n_keys % 128
        ckv_all = jnp.concatenate([past[0], ckv, jnp.zeros((B, n_pad, KV_LORA), F32)], axis=1)
        kr_all = jnp.concatenate([past[1], kr, jnp.zeros((B, n_pad, MLA_ROPE), F32)], axis=1)
        kv_valid = n_keys
    nk = ckv_all.shape[1]
    if past is None:
        tk = _tile(T, TILE_ATTN_K)
        tq = _tile(T, TILE_ATTN_Q)
        kcat, vt = _kvexpand(ckv_all, kr_all, w["wk_pad"], w["e_mat"], w["wvt"], tk)
        out_b = _attn(q, kcat, vt, True, kv_valid, tq, tk)
    else:
        kcat, vt = _kvexpand(ckv_all, kr_all, w["wk_pad"], w["e_mat"], w["wvt"], nk)
        q_pad = -T % 128
        q_in = jnp.concatenate([q, jnp.zeros((B, q_pad, q.shape[-1]), q.dtype)], axis=1)
        out_b = _attn(q_in, kcat, vt, False, kv_valid, T + q_pad, nk)[:, :T]

    cc = min(CHUNK, T)
    out_c, s_new = _hgrn(hq, hf, hi, hg, lb_raw, w["gn"], s0, layer, _tile(T, TILE_HGRN), cc)

    flat = lambda a: a.reshape(B * T, a.shape[-1])
    x2 = _merge(flat(x), [flat(out_a), flat(out_b), flat(out_c), flat(out_d)], w["gpre"], w["w_gate"],
                w["b_gate"], w["wb"], w["wo"], w["gpost"], _tile(B * T, TILE_MERGE))
    x3, nbuf_f = _ffn(x2.reshape(B, T, D_MODEL), w["fpre"], w["wup"], w["cw"], w["wdn"], w["fpost"], buf_f,
                      _tile(T, TILE_FFN))
    return x3, (ckv, kr, nbuf_a, s_new, nbuf_d, nbuf_f)


def kernel(x_prompt, x_sample, cache_ckv, cache_krope, state_conv_a, state_hgrn, state_conv_d, state_ffn_conv,
           norm_mix_pre, norm_mix_post, norm_ffn_pre, norm_ffn_post, w_in, b_gate, conv_a_w,
           mla_q_norm, mla_w_uq, mla_kv_norm, mla_w_uk, mla_w_uv, hgrn_lb_raw, hgrn_norm,
           conv_d_w, conv_d_b, ln_d_g, ln_d_b, w_branch, w_out, ffn_w_up, ffn_conv_w, ffn_w_down):
    depth = w_in.shape[0]
    B, T, _ = x_prompt.shape
    Bs, Ts, _ = x_sample.shape
    past_len = cache_ckv.shape[2]
    tabs_p = _rope_tables(jnp.arange(T))
    tabs_s = _rope_tables(past_len + jnp.arange(Ts))
    zero_state = (jnp.zeros((B, A_CONV - 1, BRANCH_W), F32), jnp.zeros((B, HG_HEADS, HG_D, HG_D), F32),
                  jnp.zeros((B, D_CONV - 1, BRANCH_W), F32), jnp.zeros((B, FFN_CONV - 1, D_FF), F32))
    lb_raw = hgrn_lb_raw.astype(F32)
    yp, ys = x_prompt, x_sample
    p_states, s_states = [], []
    for l in range(depth):
        w = _layer_weights(l, norm_mix_pre, norm_mix_post, norm_ffn_pre, norm_ffn_post, w_in, b_gate, conv_a_w,
                           mla_q_norm, mla_w_uq, mla_kv_norm, mla_w_uk, mla_w_uv, hgrn_norm,
                           conv_d_w, conv_d_b, ln_d_g, ln_d_b, w_branch, w_out, ffn_w_up, ffn_conv_w,
                           ffn_w_down)
        yp, sp = _layer(yp, tabs_p, None, zero_state, w, lb_raw, l)
        ys, ss = _layer(ys, tabs_s, (cache_ckv[l], cache_krope[l]),
                        (state_conv_a[l], state_hgrn[l], state_conv_d[l], state_ffn_conv[l]), w, lb_raw, l)
        p_states.append(sp)
        s_states.append(ss)
    ps = [jnp.stack(z) for z in zip(*p_states)]
    sst = [jnp.stack(z) for z in zip(*s_states)]
    return (yp, ys, ps[0], ps[1], ps[2], ps[3], ps[4], ps[5], sst[0], sst[1], sst[2], sst[3], sst[4], sst[5])
```

```python
import functools

import numpy as np
import jax
import jax.numpy as jnp
from jax import lax
from jax.experimental import pallas as pl
from jax.experimental.pallas import tpu as pltpu

D_MODEL = 1024
BRANCH_W = 512
N_BRANCH = 4
MLA_HEADS = 8
MLA_NOPE = 64
MLA_ROPE = 32
MLA_V = 64
Q_LORA = 384
KV_LORA = 256
ROPE_BASE = 10000.0
HG_HEADS = 4
HG_D = 128
A_CONV = 3
D_CONV = 31
FFN_CONV = 3
D_FF = 2816
CHUNK = 64
EPS = 1e-6
LOG2E = 1.4426950408889634
HEAD_PAD = 128
HG_CHUNK = 128
SUB = 8
D_GROUP = 128

_C_A = 0
_C_CQ = 3 * BRANCH_W
_C_CKV = _C_CQ + Q_LORA
_C_HG = _C_CKV + KV_LORA
_C_DIN = _C_HG + 4 * BRANCH_W
_C_KR = _C_DIN + 2 * BRANCH_W
_C_END = _C_KR + 2 * MLA_ROPE

TILE_PROJ = 512
TILE_ATTN_Q = 1024
TILE_ATTN_K = 512
ATTN_STRIP = 256
ATTN_LOOKAHEAD = 3
L_ROWS = 16
TILE_HGRN = 512
TILE_MERGE = 512
TILE_FFN = 512
VMEM_LIMIT = 56 * 1024 * 1024

F32 = jnp.float32
BF16 = jnp.bfloat16


def _rms(x, g):
    return x * lax.rsqrt(jnp.mean(x * x, axis=-1, keepdims=True) + EPS) * g


def _sigmoid(x):
    return 1.0 / (1.0 + jnp.exp(-x))


def _silu(x):
    return x * _sigmoid(x)


def _dot(a, b):
    return jnp.dot(a, b, preferred_element_type=F32)


def _const_spec(shape):
    nd = len(shape)
    return pl.BlockSpec(shape, lambda *_: (0,) * nd, pipeline_mode=pl.Buffered(1))


def _params(n_axes):
    return pltpu.CompilerParams(dimension_semantics=("arbitrary",) * n_axes,
                                vmem_limit_bytes=VMEM_LIMIT)


def _proj_kernel(x_ref, gpre_ref, w_ref, conva_ref, qn_ref, wuq_ref, kvn_ref,
                 cdw_ref, cdb_ref, lng_ref, lnb_ref, bufa_ref, bufd_ref,
                 cq_ref, sq_ref, ck_ref, sk_ref, wk_ref, e_ref, wvt_ref,
                 outa_ref, outd_ref, q_ref, ckv_ref, kr_ref,
                 hq_ref, hf_ref, hi_ref, hg_ref, nbufa_ref, nbufd_ref,
                 *rest, tm, emit_kv):
    if emit_kv:
        k_ref, vt_ref, ua_s, gd_s = rest
    else:
        ua_s, gd_s = rest
    t = pl.program_id(1)

    @pl.when(t == 0)
    def _():
        ua_s[0:8, :] = jnp.zeros((8, BRANCH_W), F32)
        ua_s[8 - (A_CONV - 1):8, :] = bufa_ref[0]
        gd_s[0:32, :] = jnp.zeros((32, BRANCH_W), F32)
        gd_s[32 - (D_CONV - 1):32, :] = bufd_ref[0]
        gd_s[tm + 32:tm + 40, :] = jnp.zeros((8, BRANCH_W), F32)

    h = _rms(x_ref[0], gpre_ref[...]).astype(BF16)

    base = 32 - (D_CONV - 1)

    def glu_group(g):
        pd = _dot(h, w_ref[:, _C_DIN + 2 * D_GROUP * g:_C_DIN + 2 * D_GROUP * (g + 1)])
        gd_s[32:32 + tm, D_GROUP * g:D_GROUP * (g + 1)] = pd[:, :D_GROUP] * _sigmoid(pd[:, D_GROUP:])

    def conv_group(g):
        cs = slice(D_GROUP * g, D_GROUP * (g + 1))
        y = cdb_ref[:, cs]
        for r in range(8):
            part = None
            for k in [k for k in range(D_CONV) if (base + k) % 8 == r]:
                lo = base + k - r
                term = cdw_ref[k:k + 1, cs] * gd_s[lo:lo + tm + 8, cs]
                part = term if part is None else part + term
            y = y + part[r:r + tm, :]
        return y

    glu_group(0)
    a_b = _dot(h, w_ref[:, _C_A:_C_A + BRANCH_W])
    yd0 = conv_group(0)
    a_c = _dot(h, w_ref[:, _C_A + BRANCH_W:_C_A + 2 * BRANCH_W])
    glu_group(1)
    a_x = _dot(h, w_ref[:, _C_A + 2 * BRANCH_W:_C_CQ])
    u = a_c * a_x
    ua_s[8:8 + tm, :] = u
    ya = (conva_ref[0:1, :] * ua_s[6:6 + tm, :] + conva_ref[1:2, :] * ua_s[7:7 + tm, :]
          + conva_ref[2:3, :] * u)
    outa_ref[0] = (a_b * ya).astype(BF16)
    nbufa_ref[0] = ua_s[tm + 6:tm + 8, :]
    ua_s[0:8, :] = ua_s[tm:tm + 8, :]
    yd1 = conv_group(1)

    pq = _dot(h, w_ref[:, _C_CQ:_C_HG])
    cqn = _rms(pq[:, :Q_LORA], qn_ref[...]).astype(BF16)
    ckv = _rms(pq[:, Q_LORA:], kvn_ref[...])
    ckv_ref[0] = ckv
    glu_group(2)
    qab = _dot(cqn, wuq_ref[...])
    cq = cq_ref[...]
    sq = sq_ref[...]
    nq = MLA_HEADS * HEAD_PAD
    for hh in range(MLA_HEADS):
        lo = hh * HEAD_PAD
        q_ref[0, :, lo:lo + HEAD_PAD] = (qab[:, lo:lo + HEAD_PAD] * cq
                                         + qab[:, nq + lo:nq + lo + HEAD_PAD] * sq).astype(BF16)
    yd2 = conv_group(2)
    pk = _dot(h, w_ref[:, _C_KR:_C_END])
    kr = pk[:, :MLA_ROPE] * ck_ref[...] + pk[:, MLA_ROPE:] * sk_ref[...]
    kr_ref[0] = kr
    if emit_kv:
        c = ckv.astype(BF16)
        k_ref[0] = (_dot(c, wk_ref[...]) + _dot(kr.astype(BF16), e_ref[...])).astype(BF16)
        vt_ref[0, 0] = lax.dot_general(wvt_ref[...], c, (((1,), (1,)), ((), ())),
                                       preferred_element_type=F32).astype(BF16)

    hq_ref[0] = _dot(h, w_ref[:, _C_HG:_C_HG + BRANCH_W]).astype(BF16)
    glu_group(3)
    hf_ref[0] = _dot(h, w_ref[:, _C_HG + BRANCH_W:_C_HG + 2 * BRANCH_W])
    yd3 = conv_group(3)
    hi_ref[0] = _dot(h, w_ref[:, _C_HG + 2 * BRANCH_W:_C_HG + 3 * BRANCH_W]).astype(BF16)
    hg_ref[0] = _dot(h, w_ref[:, _C_HG + 3 * BRANCH_W:_C_DIN]).astype(BF16)

    yd = jnp.concatenate([yd0, yd1, yd2, yd3], axis=-1)
    mu = jnp.mean(yd, axis=-1, keepdims=True)
    yc = yd - mu
    ln = yc * lax.rsqrt(jnp.mean(yc * yc, axis=-1, keepdims=True) + EPS) * lng_ref[...] + lnb_ref[...]
    outd_ref[0] = _silu(ln).astype(BF16)
    nbufd_ref[0] = gd_s[tm + 2:tm + 32, :]
    gd_s[0:32, :] = gd_s[tm:tm + 32, :]


def _proj(x, gpre, w_small, conva, qn, wuq, kvn, cdw, cdb, lng, lnb, bufa, bufd, tabs, wk_pad, e_mat, wvt,
          tm, emit_kv):
    B, T, _ = x.shape
    cq, sq, ck, sk = tabs
    nt = T // tm
    row = lambda w: pl.BlockSpec((1, tm, w), lambda b, t: (b, t, 0))
    tab = lambda w: pl.BlockSpec((tm, w), lambda b, t: (t, 0))
    st = lambda r: pl.BlockSpec((1, r, BRANCH_W), lambda b, t: (b, 0, 0))
    in_specs = [row(D_MODEL), _const_spec((1, D_MODEL)), _const_spec(w_small.shape),
                _const_spec(conva.shape), _const_spec((1, Q_LORA)), _const_spec(wuq.shape),
                _const_spec((1, KV_LORA)), _const_spec(cdw.shape), _const_spec((1, BRANCH_W)),
                _const_spec((1, BRANCH_W)), _const_spec((1, BRANCH_W)),
                st(A_CONV - 1), st(D_CONV - 1),
                tab(HEAD_PAD), tab(HEAD_PAD), tab(MLA_ROPE), tab(MLA_ROPE),
                _const_spec(wk_pad.shape), _const_spec(e_mat.shape), _const_spec(wvt.shape)]
    out_shape = [jax.ShapeDtypeStruct((B, T, BRANCH_W), BF16),
                 jax.ShapeDtypeStruct((B, T, BRANCH_W), BF16),
                 jax.ShapeDtypeStruct((B, T, MLA_HEADS * HEAD_PAD), BF16),
                 jax.ShapeDtypeStruct((B, T, KV_LORA), F32),
                 jax.ShapeDtypeStruct((B, T, MLA_ROPE), F32),
                 jax.ShapeDtypeStruct((B, T, BRANCH_W), BF16),
                 jax.ShapeDtypeStruct((B, T, BRANCH_W), F32),
                 jax.ShapeDtypeStruct((B, T, BRANCH_W), BF16),
                 jax.ShapeDtypeStruct((B, T, BRANCH_W), BF16),
                 jax.ShapeDtypeStruct((B, A_CONV - 1, BRANCH_W), F32),
                 jax.ShapeDtypeStruct((B, D_CONV - 1, BRANCH_W), F32)]
    out_specs = [row(BRANCH_W), row(BRANCH_W), row(MLA_HEADS * HEAD_PAD), row(KV_LORA), row(MLA_ROPE),
                 row(BRANCH_W), row(BRANCH_W), row(BRANCH_W), row(BRANCH_W),
                 st(A_CONV - 1), st(D_CONV - 1)]
    if emit_kv:
        out_shape += [jax.ShapeDtypeStruct((B, T, MLA_HEADS * HEAD_PAD), BF16),
                      jax.ShapeDtypeStruct((B, nt, MLA_HEADS * MLA_V, tm), BF16)]
        out_specs += [row(MLA_HEADS * HEAD_PAD),
                      pl.BlockSpec((1, 1, MLA_HEADS * MLA_V, tm), lambda b, t: (b, t, 0, 0))]
    return pl.pallas_call(
        functools.partial(_proj_kernel, tm=tm, emit_kv=emit_kv),
        grid=(B, nt), in_specs=in_specs, out_specs=out_specs, out_shape=out_shape,
        scratch_shapes=[pltpu.VMEM((tm + 8, BRANCH_W), F32), pltpu.VMEM((tm + 40, BRANCH_W), F32)],
        compiler_params=_params(2), name="proj",
    )(x, gpre, w_small, conva, qn, wuq, kvn, cdw, cdb, lng, lnb, bufa, bufd, cq, sq, ck, sk,
      wk_pad, e_mat, wvt)


def _kvexpand_kernel(ckv_ref, kr_ref, wk_ref, e_ref, wvt_ref, k_ref, vt_ref):
    c = ckv_ref[0].astype(BF16)
    k_ref[0] = (_dot(c, wk_ref[...]) + _dot(kr_ref[0].astype(BF16), e_ref[...])).astype(BF16)
    vt_ref[0, 0] = lax.dot_general(wvt_ref[...], c, (((1,), (1,)), ((), ())),
                                   preferred_element_type=F32).astype(BF16)


def _kvexpand(ckv, kr, wk_pad, e_mat, wvt, tk):
    B, n, _ = ckv.shape
    nb = n // tk
    row = lambda w: pl.BlockSpec((1, tk, w), lambda b, j: (b, j, 0))
    return pl.pallas_call(
        _kvexpand_kernel, grid=(B, nb),
        in_specs=[row(KV_LORA), row(MLA_ROPE), _const_spec(wk_pad.shape), _const_spec(e_mat.shape),
                  _const_spec(wvt.shape)],
        out_specs=[row(MLA_HEADS * HEAD_PAD),
                   pl.BlockSpec((1, 1, MLA_HEADS * MLA_V, tk), lambda b, j: (b, j, 0, 0))],
        out_shape=[jax.ShapeDtypeStruct((B, n, MLA_HEADS * HEAD_PAD), BF16),
                   jax.ShapeDtypeStruct((B, nb, MLA_HEADS * MLA_V, tk), BF16)],
        compiler_params=_params(2), name="kvexpand",
    )(ckv, kr, wk_pad, e_mat, wvt)


def _attn_kernel(q_ref, k_ref, vt_ref, o_ref, qt_s, m_s, acc_s, *, causal, kv_valid, tq, tk):
    i = pl.program_id(1)
    strip = min(ATTN_STRIP, tq)
    qt_s[...] = q_ref[0].astype(F32).T.astype(BF16)
    m_s[...] = jnp.full(m_s.shape, -jnp.inf, F32)
    acc_s[...] = jnp.zeros(acc_s.shape, F32)
    ones_rows = jnp.ones((L_ROWS, tk), BF16)
    all_units = [(hh, c) for hh in range(MLA_HEADS) for c in range(tq // strip)]

    def block(j, mask, first_strip=0):
        units = [u for u in all_units if u[1] >= first_strip]
        r0 = j * tk
        if not isinstance(r0, int):
            r0 = pl.multiple_of(r0, tk)

        def scores(u):
            hh, c = u
            kh = k_ref[0, pl.ds(r0, tk), hh * HEAD_PAD:(hh + 1) * HEAD_PAD]
            return _dot(kh, qt_s[hh * HEAD_PAD:(hh + 1) * HEAD_PAD, c * strip:(c + 1) * strip])

        pend = [scores(u) for u in units[:ATTN_LOOKAHEAD]]
        for n, (hh, c) in enumerate(units):
            cs = slice(c * strip, (c + 1) * strip)
            s = pend.pop(0)
            if mask is not None:
                s = jnp.where(mask[:, cs], s, -jnp.inf)
            m_prev = m_s[hh:hh + 1, cs]
            m_new = jnp.maximum(m_prev, jnp.max(s, axis=0, keepdims=True))
            alpha = jnp.exp2(m_prev - m_new)
            p = jnp.exp2(s - m_new).astype(BF16)
            if n + ATTN_LOOKAHEAD < len(units):
                pend.append(scores(units[n + ATTN_LOOKAHEAD]))
            va = jnp.concatenate([vt_ref[0, j, hh * MLA_V:(hh + 1) * MLA_V, :], ones_rows], axis=0)
            acc_s[hh, :, cs] = alpha * acc_s[hh, :, cs] + _dot(va, p)
            m_s[hh:hh + 1, cs] = m_new

    if causal:
        ratio = tq // tk

        def full_block(j, carry):
            block(j, None)
            return carry

        lax.fori_loop(0, i * ratio, full_block, 0)
        qc = lax.broadcasted_iota(jnp.int32, (tk, tq), 1) // CHUNK
        for r in range(ratio):
            kc = (r * tk + lax.broadcasted_iota(jnp.int32, (tk, tq), 0)) // CHUNK
            block(i * ratio + r, kc <= qc, first_strip=(r * tk) // strip)
    else:
        nkv = k_ref.shape[1] // tk
        for j in range(nkv):
            if (j + 1) * tk > kv_valid:
                block(j, j * tk + lax.broadcasted_iota(jnp.int32, (tk, tq), 0) < kv_valid)
            else:
                block(j, None)

    outs = [acc_s[hh, :MLA_V, :] / acc_s[hh, MLA_V:MLA_V + 1, :] for hh in range(MLA_HEADS)]
    o_ref[0] = jnp.concatenate(outs, axis=0).T.astype(BF16)


def _attn(q, k, vt, causal, kv_valid, tq, tk):
    B, T, _ = q.shape
    nk = k.shape[1]
    return pl.pallas_call(
        functools.partial(_attn_kernel, causal=causal, kv_valid=kv_valid, tq=tq, tk=tk),
        grid=(B, T // tq),
        in_specs=[pl.BlockSpec((1, tq, MLA_HEADS * HEAD_PAD), lambda b, i: (b, i, 0)),
                  pl.BlockSpec((1, nk, MLA_HEADS * HEAD_PAD), lambda b, i: (b, 0, 0),
                               pipeline_mode=pl.Buffered(1)),
                  pl.BlockSpec((1, nk // tk, MLA_HEADS * MLA_V, tk), lambda b, i: (b, 0, 0, 0),
                               pipeline_mode=pl.Buffered(1))],
        out_specs=pl.BlockSpec((1, tq, MLA_HEADS * MLA_V), lambda b, i: (b, i, 0)),
        out_shape=jax.ShapeDtypeStruct((B, T, MLA_HEADS * MLA_V), BF16),
        scratch_shapes=[pltpu.VMEM((MLA_HEADS * HEAD_PAD, tq), BF16), pltpu.VMEM((MLA_HEADS, tq), F32),
                        pltpu.VMEM((MLA_HEADS, MLA_V + L_ROWS, tq), F32)],
        compiler_params=_params(2), name="attn",
    )(q, k, vt)


def _hgrn_kernel(hq_ref, hf_ref, hi_ref, hg_ref, lbraw_ref, gn_ref, s0_ref,
                 o_ref, sfin_ref, st_s, b2_s, c2_s, *, layer, tm, cc):
    t = pl.program_id(1)
    nt = pl.num_programs(1)
    W = HG_HEADS * HG_D

    @pl.when(t == 0)
    def _():
        for hh in range(HG_HEADS):
            st_s[hh] = s0_ref[0, hh].T

    raw = lbraw_ref[...]
    e = jnp.exp(raw - jnp.max(raw, axis=0, keepdims=True))
    pr = e / jnp.sum(e, axis=0, keepdims=True)
    cum = pr[0:1, :]
    for r in range(1, layer + 1):
        cum = cum + pr[r:r + 1, :]
    lb = cum - pr[0:1, :]
    log_lb = jnp.log(lb)
    log_1mlb = jnp.log1p(-lb)
    gn = gn_ref[...]

    tri = (lax.broadcasted_iota(jnp.int32, (cc, cc), 0)
           >= lax.broadcasted_iota(jnp.int32, (cc, cc), 1)).astype(BF16)
    lane_c = lax.broadcasted_iota(jnp.int32, (SUB, cc), 1)
    row_s = lax.broadcasted_iota(jnp.int32, (SUB, 1), 0)
    nsub = cc // SUB

    for c in range(tm // cc):
        rows = slice(c * cc, (c + 1) * cc)
        zf = hf_ref[0, rows, :]
        ls = jnp.minimum(zf, 0.0) - jnp.log(1.0 + jnp.exp(-jnp.abs(zf)))
        bb = log_1mlb + ls
        g = jnp.maximum(log_lb, bb) + jnp.log(1.0 + jnp.exp(-jnp.abs(log_lb - bb)))
        g1 = g.astype(BF16)
        r1 = g - g1.astype(F32)
        g2 = r1.astype(BF16)
        g3 = (r1 - g2.astype(F32)).astype(BF16)
        b2 = (_dot(tri, g1) + _dot(tri, g2) + _dot(tri, g3)) * LOG2E
        b2_s[rows, :] = b2
        c2_s[rows, :] = b2 - (bb - zf) * LOG2E

    def chunk(c, carry):
        r0 = pl.multiple_of(c * cc, cc)
        q = hq_ref[0, pl.ds(r0, cc), :].astype(F32)
        v = hi_ref[0, pl.ds(r0, cc), :]
        b2 = b2_s[pl.ds(r0, cc), :]
        c2 = c2_s[pl.ds(r0, cc), :]
        bl2 = b2[cc - 1:cc, :]
        qb = (q * jnp.exp2(b2)).astype(BF16)
        kdec = jnp.exp2(bl2 - c2).astype(BF16)
        ebl = jnp.exp2(bl2)

        o_inter = []
        for hh in range(HG_HEADS):
            sl = slice(hh * HG_D, (hh + 1) * HG_D)
            st = st_s[hh]
            o_inter.append(lax.dot_general(qb[:, sl], st.astype(BF16), (((1,), (1,)), ((), ())),
                                           preferred_element_type=F32))
            st_s[hh] = st * ebl[:, sl] + lax.dot_general(v[:, sl], kdec[:, sl], (((0,), (0,)), ((), ())),
                                                         preferred_element_type=F32)

        a_rows = []
        for si in range(nsub):
            lo = si * SUB
            q_i = q[lo:lo + SUB, :]
            b_i = b2[lo:lo + SUB, :]
            if si > 0:
                ref = b2[lo - 1:lo, :]
                qt = (q_i * jnp.exp2(b_i - ref)).astype(BF16)
                kt = jnp.concatenate([jnp.exp2(ref - c2[:lo, :]),
                                      jnp.zeros((cc - lo, W), F32)], axis=0).astype(BF16)
            a_h = []
            for hh in range(HG_HEADS):
                sl = slice(hh * HG_D, (hh + 1) * HG_D)
                if si > 0:
                    a = lax.dot_general(qt[:, sl], kt[:, sl], (((1,), (1,)), ((), ())),
                                        preferred_element_type=F32)
                else:
                    a = jnp.zeros((SUB, cc), F32)
                a_h.append(a)
            for s in range(SUB):
                z = q_i * jnp.exp2(b_i - c2[lo + s:lo + s + 1, :])
                place = (lane_c == lo + s) & (row_s >= s)
                for hh in range(HG_HEADS):
                    col = jnp.sum(z[:, hh * HG_D:(hh + 1) * HG_D], axis=-1, keepdims=True)
                    a_h[hh] = jnp.where(place, col, a_h[hh])
            a_rows.append(a_h)

        outs = []
        for hh in range(HG_HEADS):
            sl = slice(hh * HG_D, (hh + 1) * HG_D)
            a_full = jnp.concatenate([a_rows[si][hh] for si in range(nsub)], axis=0).astype(BF16)
            o = o_inter[hh] + _dot(a_full, v[:, sl])
            outs.append(o * lax.rsqrt(jnp.mean(o * o, axis=-1, keepdims=True) + EPS))
        on = jnp.concatenate(outs, axis=-1) * gn
        o_ref[0, pl.ds(r0, cc), :] = (on * _silu(hg_ref[0, pl.ds(r0, cc), :].astype(F32))).astype(BF16)
        return carry

    lax.fori_loop(0, tm // cc, chunk, 0)

    @pl.when(t == nt - 1)
    def _():
        for hh in range(HG_HEADS):
            sfin_ref[0, hh] = st_s[hh].T


def _hgrn(hq, hf, hi, hg, lb_raw, gn, s0, layer, tm, cc):
    B, T, W = hq.shape
    row = pl.BlockSpec((1, tm, W), lambda b, t: (b, t, 0))
    st = pl.BlockSpec((1, HG_HEADS, HG_D, HG_D), lambda b, t: (b, 0, 0, 0))
    return pl.pallas_call(
        functools.partial(_hgrn_kernel, layer=layer, tm=tm, cc=cc),
        grid=(B, T // tm),
        in_specs=[row, row, row, row, _const_spec(lb_raw.shape), _const_spec((1, W)), st],
        out_specs=[row, st],
        out_shape=[jax.ShapeDtypeStruct((B, T, W), BF16),
                   jax.ShapeDtypeStruct((B, HG_HEADS, HG_D, HG_D), F32)],
        scratch_shapes=[pltpu.VMEM((HG_HEADS, HG_D, HG_D), F32), pltpu.VMEM((tm, W), F32),
                        pltpu.VMEM((tm, W), F32)],
        compiler_params=_params(2), name="hgrn",
    )(hq, hf, hi, hg, lb_raw, gn, s0)


def _merge_kernel(x_ref, a_ref, b_ref, c_ref, d_ref, gpre_ref, wg_ref, bg_ref, wb_ref, wo_ref, gpost_ref,
                  y_ref):
    x = x_ref[...]
    h = _rms(x, gpre_ref[...]).astype(BF16)
    merged = None
    for i, br in enumerate((a_ref, b_ref, c_ref, d_ref)):
        gate = _sigmoid(_dot(h, wg_ref[:, i * D_MODEL:(i + 1) * D_MODEL])
                        + bg_ref[:, i * D_MODEL:(i + 1) * D_MODEL])
        term = gate * _dot(br[...], wb_ref[i])
        merged = term if merged is None else merged + term
    y = _dot(merged.astype(BF16), wo_ref[...])
    y_ref[...] = x + _rms(y, gpost_ref[...])


def _merge(x, br, gpre, wg, bg, wb, wo, gpost, tm):
    n = x.shape[0]
    row = lambda w: pl.BlockSpec((tm, w), lambda i: (i, 0))
    return pl.pallas_call(
        _merge_kernel, grid=(n // tm,),
        in_specs=[row(D_MODEL)] + [row(BRANCH_W)] * 4
                 + [_const_spec((1, D_MODEL)), _const_spec(wg.shape), _const_spec(bg.shape),
                    _const_spec(wb.shape), _const_spec(wo.shape), _const_spec((1, D_MODEL))],
        out_specs=row(D_MODEL),
        out_shape=jax.ShapeDtypeStruct((n, D_MODEL), F32),
        compiler_params=_params(1), name="merge",
    )(x, *br, gpre, wg, bg, wb, wo, gpost)


def _ffn_kernel(x_ref, gpre_ref, wup_ref, cw_ref, wdn_ref, gpost_ref, buf_ref,
                y_ref, nbuf_ref, g_s, *, tm, nsplit):
    t = pl.program_id(1)

    @pl.when(t == 0)
    def _():
        g_s[0:8, :] = jnp.zeros((8, D_FF), F32)
        g_s[8 - (FFN_CONV - 1):8, :] = buf_ref[0]

    x = x_ref[0]
    h = _rms(x, gpre_ref[...]).astype(BF16)
    wc = D_FF // nsplit
    ff = None
    for s in range(nsplit):
        lo = s * wc
        g = _dot(h, wup_ref[:, lo:lo + wc])
        g_s[8:8 + tm, lo:lo + wc] = g
        fg = (cw_ref[0:1, lo:lo + wc] * g_s[6:6 + tm, lo:lo + wc]
              + cw_ref[1:2, lo:lo + wc] * g_s[7:7 + tm, lo:lo + wc]
              + cw_ref[2:3, lo:lo + wc] * g)
        up = _dot(h, wup_ref[:, D_FF + lo:D_FF + lo + wc])
        part = _dot((_silu(fg) * up).astype(BF16), wdn_ref[lo:lo + wc, :])
        ff = part if ff is None else ff + part
    y_ref[0] = x + _rms(ff, gpost_ref[...])
    nbuf_ref[0] = g_s[tm + 6:tm + 8, :]
    g_s[0:8, :] = g_s[tm:tm + 8, :]


def _ffn(x, gpre, wup, cw, wdn, gpost, buf, tm):
    B, T, _ = x.shape
    row = pl.BlockSpec((1, tm, D_MODEL), lambda b, t: (b, t, 0))
    st = pl.BlockSpec((1, FFN_CONV - 1, D_FF), lambda b, t: (b, 0, 0))
    return pl.pallas_call(
        functools.partial(_ffn_kernel, tm=tm, nsplit=2),
        grid=(B, T // tm),
        in_specs=[row, _const_spec((1, D_MODEL)), _const_spec(wup.shape), _const_spec(cw.shape),
                  _const_spec(wdn.shape), _const_spec((1, D_MODEL)), st],
        out_specs=[row, st],
        out_shape=[jax.ShapeDtypeStruct((B, T, D_MODEL), F32),
                   jax.ShapeDtypeStruct((B, FFN_CONV - 1, D_FF), F32)],
        scratch_shapes=[pltpu.VMEM((tm + 8, D_FF), F32)],
        compiler_params=_params(2), name="ffn",
    )(x, gpre, wup, cw, wdn, gpost, buf)


def _tile(n, pref):
    return pref if n % pref == 0 else n


def _rope_tables(pos):
    half = MLA_ROPE // 2
    inv = ROPE_BASE ** (-jnp.arange(half, dtype=F32) / half)
    ang = pos.astype(F32)[:, None] * inv[None, :]
    cos, sin = jnp.cos(ang), jnp.sin(ang)
    n = pos.shape[0]
    scale = (MLA_NOPE + MLA_ROPE) ** -0.5 * np.log2(np.e)
    pad = jnp.zeros((n, HEAD_PAD - MLA_NOPE - MLA_ROPE), F32)
    cq = scale * jnp.concatenate([jnp.ones((n, MLA_NOPE), F32), cos, cos, pad], axis=1)
    sq = scale * jnp.concatenate([jnp.zeros((n, MLA_NOPE), F32), -sin, sin, pad], axis=1)
    ck = jnp.concatenate([cos, cos], axis=1)
    sk = jnp.concatenate([-sin, sin], axis=1)
    return cq, sq, ck, sk


def _layer_weights(l, norm_mix_pre, norm_mix_post, norm_ffn_pre, norm_ffn_post, w_in, b_gate, conv_a_w,
                   mla_q_norm, mla_w_uq, mla_kv_norm, mla_w_uk, mla_w_uv, hgrn_norm,
                   conv_d_w, conv_d_b, ln_d_g, ln_d_b, w_branch, w_out, ffn_w_up, ffn_conv_w, ffn_w_down):
    wi = w_in[l]
    o_kr = 3 * BRANCH_W + Q_LORA + KV_LORA
    o_h = o_kr + MLA_ROPE
    o_g = o_h + 6 * BRANCH_W
    half = MLA_ROPE // 2
    o_d = o_h + 4 * BRANCH_W
    n_dg = BRANCH_W // D_GROUP
    w_din = wi[:, o_d:o_g].reshape(D_MODEL, 2, n_dg, D_GROUP).transpose(0, 2, 1, 3).reshape(D_MODEL, 2 * BRANCH_W)
    w_small = jnp.concatenate(
        [wi[:, :o_kr], wi[:, o_h:o_d], w_din, wi[:, o_kr:o_h], wi[:, o_kr + half:o_h], wi[:, o_kr:o_kr + half]],
        axis=1).astype(BF16)
    w_gate = wi[:, o_g:].astype(BF16)
    uq = mla_w_uq[l].reshape(Q_LORA, MLA_HEADS, MLA_NOPE + MLA_ROPE)
    nope, r1, r2 = uq[..., :MLA_NOPE], uq[..., MLA_NOPE:MLA_NOPE + half], uq[..., MLA_NOPE + half:]
    z_pad = jnp.zeros((Q_LORA, MLA_HEADS, HEAD_PAD - MLA_NOPE - MLA_ROPE), F32)
    z_nope = jnp.zeros((Q_LORA, MLA_HEADS, MLA_NOPE), F32)
    wuq = jnp.concatenate(
        [jnp.concatenate([nope, r1, r2, z_pad], axis=-1).reshape(Q_LORA, MLA_HEADS * HEAD_PAD),
         jnp.concatenate([z_nope, r2, r1, z_pad], axis=-1).reshape(Q_LORA, MLA_HEADS * HEAD_PAD)],
        axis=1).astype(BF16)
    uk = mla_w_uk[l].reshape(KV_LORA, MLA_HEADS, MLA_NOPE)
    wk_pad = jnp.concatenate([uk, jnp.zeros((KV_LORA, MLA_HEADS, HEAD_PAD - MLA_NOPE), F32)],
                             axis=-1).reshape(KV_LORA, MLA_HEADS * HEAD_PAD).astype(BF16)
    e_np = np.zeros((MLA_ROPE, MLA_HEADS, HEAD_PAD), np.float32)
    for r in range(MLA_ROPE):
        e_np[r, :, MLA_NOPE + r] = 1.0
    e_mat = jnp.asarray(e_np.reshape(MLA_ROPE, MLA_HEADS * HEAD_PAD), BF16)
    r2d = lambda a: a[l].reshape(1, -1)
    return dict(
        gpre=r2d(norm_mix_pre), gpost=r2d(norm_mix_post), fpre=r2d(norm_ffn_pre), fpost=r2d(norm_ffn_post),
        w_small=w_small, w_gate=w_gate, b_gate=r2d(b_gate), conva=conv_a_w[l], qn=r2d(mla_q_norm), wuq=wuq,
        kvn=r2d(mla_kv_norm), wk_pad=wk_pad, e_mat=e_mat, wvt=mla_w_uv[l].T.astype(BF16), gn=r2d(hgrn_norm),
        cdw=conv_d_w[l], cdb=r2d(conv_d_b), lng=r2d(ln_d_g), lnb=r2d(ln_d_b),
        wb=w_branch[l].astype(BF16), wo=w_out[l].astype(BF16), wup=ffn_w_up[l].astype(BF16),
        cw=ffn_conv_w[l], wdn=ffn_w_down[l].astype(BF16))


def _layer(x, tabs, past, state, w, lb_raw, layer):
    B, T, _ = x.shape
    buf_a, s0, buf_d, buf_f = state
    prompt = past is None
    tk = _tile(T, TILE_ATTN_K)
    outs = _proj(x, w["gpre"], w["w_small"], w["conva"], w["qn"], w["wuq"], w["kvn"], w["cdw"], w["cdb"],
                 w["lng"], w["lnb"], buf_a, buf_d, tabs, w["wk_pad"], w["e_mat"], w["wvt"],
                 tk if prompt else _tile(T, TILE_PROJ), prompt)
    (out_a, out_d, q, ckv, kr, hq, hf, hi, hg, nbuf_a, nbuf_d) = outs[:11]

    if prompt:
        kcat, vt = outs[11:]
        out_b = _attn(q, kcat, vt, True, T, _tile(T, TILE_ATTN_Q), tk)
    else:
        n_keys = past[0].shape[1] + T
        n_pad = -n_keys % 128
        ckv_all = jnp.concatenate([past[0], ckv, jnp.zeros((B, n_pad, KV_LORA), F32)], axis=1)
        kr_all = jnp.concatenate([past[1], kr, jnp.zeros((B, n_pad, MLA_ROPE), F32)], axis=1)
        nk = n_keys + n_pad
        kcat, vt = _kvexpand(ckv_all, kr_all, w["wk_pad"], w["e_mat"], w["wvt"], nk)
        q_pad = -T % 128
        q_in = jnp.concatenate([q, jnp.zeros((B, q_pad, q.shape[-1]), q.dtype)], axis=1)
        out_b = _attn(q_in, kcat, vt, False, n_keys, T + q_pad, nk)[:, :T]

    cc = min(HG_CHUNK, T)
    out_c, s_new = _hgrn(hq, hf, hi, hg, lb_raw, w["gn"], s0, layer, _tile(T, TILE_HGRN), cc)

    flat = lambda a: a.reshape(B * T, a.shape[-1])
    x2 = _merge(flat(x), [flat(out_a), flat(out_b), flat(out_c), flat(out_d)], w["gpre"], w["w_gate"],
                w["b_gate"], w["wb"], w["wo"], w["gpost"], _tile(B * T, TILE_MERGE))
    x3, nbuf_f = _ffn(x2.reshape(B, T, D_MODEL), w["fpre"], w["wup"], w["cw"], w["wdn"], w["fpost"], buf_f,
                      _tile(T, TILE_FFN))
    return x3, (ckv, kr, nbuf_a, s_new, nbuf_d, nbuf_f)


def kernel(x_prompt, x_sample, cache_ckv, cache_krope, state_conv_a, state_hgrn, state_conv_d, state_ffn_conv,
           norm_mix_pre, norm_mix_post, norm_ffn_pre, norm_ffn_post, w_in, b_gate, conv_a_w,
           mla_q_norm, mla_w_uq, mla_kv_norm, mla_w_uk, mla_w_uv, hgrn_lb_raw, hgrn_norm,
           conv_d_w, conv_d_b, ln_d_g, ln_d_b, w_branch, w_out, ffn_w_up, ffn_conv_w, ffn_w_down):
    depth = w_in.shape[0]
    B, T, _ = x_prompt.shape
    Bs, Ts, _ = x_sample.shape
    past_len = cache_ckv.shape[2]
    tabs_p = _rope_tables(jnp.arange(T))
    tabs_s = _rope_tables(past_len + jnp.arange(Ts))
    zero_state = (jnp.zeros((B, A_CONV - 1, BRANCH_W), F32), jnp.zeros((B, HG_HEADS, HG_D, HG_D), F32),
                  jnp.zeros((B, D_CONV - 1, BRANCH_W), F32), jnp.zeros((B, FFN_CONV - 1, D_FF), F32))
    lb_raw = hgrn_lb_raw.astype(F32)
    yp, ys = x_prompt, x_sample
    p_states, s_states = [], []
    for l in range(depth):
        w = _layer_weights(l, norm_mix_pre, norm_mix_post, norm_ffn_pre, norm_ffn_post, w_in, b_gate, conv_a_w,
                           mla_q_norm, mla_w_uq, mla_kv_norm, mla_w_uk, mla_w_uv, hgrn_norm,
                           conv_d_w, conv_d_b, ln_d_g, ln_d_b, w_branch, w_out, ffn_w_up, ffn_conv_w,
                           ffn_w_down)
        yp, sp = _layer(yp, tabs_p, None, zero_state, w, lb_raw, l)
        ys, ss = _layer(ys, tabs_s, (cache_ckv[l], cache_krope[l]),
                        (state_conv_a[l], state_hgrn[l], state_conv_d[l], state_ffn_conv[l]), w, lb_raw, l)
        p_states.append(sp)
        s_states.append(ss)
    ps = [jnp.stack(z) for z in zip(*p_states)]
    sst = [jnp.stack(z) for z in zip(*s_states)]
    return (yp, ys, ps[0], ps[1], ps[2], ps[3], ps[4], ps[5], sst[0], sst[1], sst[2], sst[3], sst[4], sst[5])
```

```python
import functools

import numpy as np
import jax
import jax.numpy as jnp
from jax import lax
from jax.experimental import pallas as pl
from jax.experimental.pallas import tpu as pltpu

D_MODEL = 1024
BRANCH_W = 512
N_BRANCH = 4
MLA_HEADS = 8
MLA_NOPE = 64
MLA_ROPE = 32
MLA_V = 64
Q_LORA = 384
KV_LORA = 256
ROPE_BASE = 10000.0
HG_HEADS = 4
HG_D = 128
A_CONV = 3
D_CONV = 31
FFN_CONV = 3
D_FF = 2816
CHUNK = 64
EPS = 1e-6
LOG2E = 1.4426950408889634
HEAD_PAD = 128
HG_CHUNK = 128
SUB = 8
D_GROUP = 128

_C_A = 0
_C_CQ = 3 * BRANCH_W
_C_CKV = _C_CQ + Q_LORA
_C_HG = _C_CKV + KV_LORA
_C_DIN = _C_HG + 4 * BRANCH_W
_C_KR = _C_DIN + 2 * BRANCH_W
_C_END = _C_KR + 2 * MLA_ROPE

TILE_PROJ = 512
TILE_ATTN_Q = 1024
TILE_ATTN_K = 256
ATTN_STRIP = 256
ATTN_LOOKAHEAD = 6
L_ROWS = 16
TILE_HGRN = 512
TILE_MERGE = 512
TILE_FFN = 512
VMEM_LIMIT = 56 * 1024 * 1024

F32 = jnp.float32
BF16 = jnp.bfloat16


def _rms(x, g):
    return x * lax.rsqrt(jnp.mean(x * x, axis=-1, keepdims=True) + EPS) * g


def _sigmoid(x):
    return 1.0 / (1.0 + jnp.exp(-x))


def _silu(x):
    return x * _sigmoid(x)


def _dot(a, b):
    return jnp.dot(a, b, preferred_element_type=F32)


def _const_spec(shape):
    nd = len(shape)
    return pl.BlockSpec(shape, lambda *_: (0,) * nd, pipeline_mode=pl.Buffered(1))


def _params(n_axes):
    return pltpu.CompilerParams(dimension_semantics=("arbitrary",) * n_axes,
                                vmem_limit_bytes=VMEM_LIMIT)


def _proj_kernel(x_ref, gpre_ref, w_ref, conva_ref, qn_ref, wuq_ref, kvn_ref,
                 cdw_ref, cdb_ref, lng_ref, lnb_ref, bufa_ref, bufd_ref,
                 cq_ref, sq_ref, ck_ref, sk_ref, wk_ref, e_ref, wvt_ref,
                 outa_ref, outd_ref, q_ref, ckv_ref, kr_ref,
                 hq_ref, hf_ref, hi_ref, hg_ref, nbufa_ref, nbufd_ref,
                 *rest, tm, kv_tile):
    if kv_tile:
        k_ref, vt_ref, ua_s, gd_s = rest
    else:
        ua_s, gd_s = rest
    t = pl.program_id(1)

    @pl.when(t == 0)
    def _():
        ua_s[0:8, :] = jnp.zeros((8, BRANCH_W), F32)
        ua_s[8 - (A_CONV - 1):8, :] = bufa_ref[0]
        gd_s[0:32, :] = jnp.zeros((32, BRANCH_W), F32)
        gd_s[32 - (D_CONV - 1):32, :] = bufd_ref[0]
        gd_s[tm + 32:tm + 40, :] = jnp.zeros((8, BRANCH_W), F32)

    h = _rms(x_ref[0], gpre_ref[...]).astype(BF16)

    base = 32 - (D_CONV - 1)

    def glu_group(g):
        pd = _dot(h, w_ref[:, _C_DIN + 2 * D_GROUP * g:_C_DIN + 2 * D_GROUP * (g + 1)])
        gd_s[32:32 + tm, D_GROUP * g:D_GROUP * (g + 1)] = pd[:, :D_GROUP] * _sigmoid(pd[:, D_GROUP:])

    def conv_group(g):
        cs = slice(D_GROUP * g, D_GROUP * (g + 1))
        y = cdb_ref[:, cs]
        for r in range(8):
            part = None
            for k in [k for k in range(D_CONV) if (base + k) % 8 == r]:
                lo = base + k - r
                term = cdw_ref[k:k + 1, cs] * gd_s[lo:lo + tm + 8, cs]
                part = term if part is None else part + term
            y = y + part[r:r + tm, :]
        return y

    glu_group(0)
    a_b = _dot(h, w_ref[:, _C_A:_C_A + BRANCH_W])
    yd0 = conv_group(0)
    a_c = _dot(h, w_ref[:, _C_A + BRANCH_W:_C_A + 2 * BRANCH_W])
    glu_group(1)
    a_x = _dot(h, w_ref[:, _C_A + 2 * BRANCH_W:_C_CQ])
    u = a_c * a_x
    ua_s[8:8 + tm, :] = u
    ya = (conva_ref[0:1, :] * ua_s[6:6 + tm, :] + conva_ref[1:2, :] * ua_s[7:7 + tm, :]
          + conva_ref[2:3, :] * u)
    outa_ref[0] = (a_b * ya).astype(BF16)
    nbufa_ref[0] = ua_s[tm + 6:tm + 8, :]
    ua_s[0:8, :] = ua_s[tm:tm + 8, :]
    yd1 = conv_group(1)

    pq = _dot(h, w_ref[:, _C_CQ:_C_HG])
    cqn = _rms(pq[:, :Q_LORA], qn_ref[...]).astype(BF16)
    ckv = _rms(pq[:, Q_LORA:], kvn_ref[...])
    ckv_ref[0] = ckv
    glu_group(2)
    qab = _dot(cqn, wuq_ref[...])
    cq = cq_ref[...]
    sq = sq_ref[...]
    nq = MLA_HEADS * HEAD_PAD
    for hh in range(MLA_HEADS):
        lo = hh * HEAD_PAD
        q_ref[0, :, lo:lo + HEAD_PAD] = (qab[:, lo:lo + HEAD_PAD] * cq
                                         + qab[:, nq + lo:nq + lo + HEAD_PAD] * sq).astype(BF16)
    yd2 = conv_group(2)
    pk = _dot(h, w_ref[:, _C_KR:_C_END])
    kr = pk[:, :MLA_ROPE] * ck_ref[...] + pk[:, MLA_ROPE:] * sk_ref[...]
    kr_ref[0] = kr
    if kv_tile:
        c = ckv.astype(BF16)
        k_ref[0] = (_dot(c, wk_ref[...]) + _dot(kr.astype(BF16), e_ref[...])).astype(BF16)
        vt = lax.dot_general(wvt_ref[...], c, (((1,), (1,)), ((), ())),
                             preferred_element_type=F32).astype(BF16)
        for i in range(tm // kv_tile):
            vt_ref[0, i] = vt[:, i * kv_tile:(i + 1) * kv_tile]

    hq_ref[0] = _dot(h, w_ref[:, _C_HG:_C_HG + BRANCH_W]).astype(BF16)
    glu_group(3)
    hf_ref[0] = _dot(h, w_ref[:, _C_HG + BRANCH_W:_C_HG + 2 * BRANCH_W])
    yd3 = conv_group(3)
    hi_ref[0] = _dot(h, w_ref[:, _C_HG + 2 * BRANCH_W:_C_HG + 3 * BRANCH_W]).astype(BF16)
    hg_ref[0] = _dot(h, w_ref[:, _C_HG + 3 * BRANCH_W:_C_DIN]).astype(BF16)

    yd = jnp.concatenate([yd0, yd1, yd2, yd3], axis=-1)
    mu = jnp.mean(yd, axis=-1, keepdims=True)
    yc = yd - mu
    ln = yc * lax.rsqrt(jnp.mean(yc * yc, axis=-1, keepdims=True) + EPS) * lng_ref[...] + lnb_ref[...]
    outd_ref[0] = _silu(ln).astype(BF16)
    nbufd_ref[0] = gd_s[tm + 2:tm + 32, :]
    gd_s[0:32, :] = gd_s[tm:tm + 32, :]


def _proj(x, gpre, w_small, conva, qn, wuq, kvn, cdw, cdb, lng, lnb, bufa, bufd, tabs, wk_pad, e_mat, wvt,
          tm, kv_tile):
    B, T, _ = x.shape
    cq, sq, ck, sk = tabs
    nt = T // tm
    row = lambda w: pl.BlockSpec((1, tm, w), lambda b, t: (b, t, 0))
    tab = lambda w: pl.BlockSpec((tm, w), lambda b, t: (t, 0))
    st = lambda r: pl.BlockSpec((1, r, BRANCH_W), lambda b, t: (b, 0, 0))
    in_specs = [row(D_MODEL), _const_spec((1, D_MODEL)), _const_spec(w_small.shape),
                _const_spec(conva.shape), _const_spec((1, Q_LORA)), _const_spec(wuq.shape),
                _const_spec((1, KV_LORA)), _const_spec(cdw.shape), _const_spec((1, BRANCH_W)),
                _const_spec((1, BRANCH_W)), _const_spec((1, BRANCH_W)),
                st(A_CONV - 1), st(D_CONV - 1),
                tab(HEAD_PAD), tab(HEAD_PAD), tab(MLA_ROPE), tab(MLA_ROPE),
                _const_spec(wk_pad.shape), _const_spec(e_mat.shape), _const_spec(wvt.shape)]
    out_shape = [jax.ShapeDtypeStruct((B, T, BRANCH_W), BF16),
                 jax.ShapeDtypeStruct((B, T, BRANCH_W), BF16),
                 jax.ShapeDtypeStruct((B, T, MLA_HEADS * HEAD_PAD), BF16),
                 jax.ShapeDtypeStruct((B, T, KV_LORA), F32),
                 jax.ShapeDtypeStruct((B, T, MLA_ROPE), F32),
                 jax.ShapeDtypeStruct((B, T, BRANCH_W), BF16),
                 jax.ShapeDtypeStruct((B, T, BRANCH_W), F32),
                 jax.ShapeDtypeStruct((B, T, BRANCH_W), BF16),
                 jax.ShapeDtypeStruct((B, T, BRANCH_W), BF16),
                 jax.ShapeDtypeStruct((B, A_CONV - 1, BRANCH_W), F32),
                 jax.ShapeDtypeStruct((B, D_CONV - 1, BRANCH_W), F32)]
    out_specs = [row(BRANCH_W), row(BRANCH_W), row(MLA_HEADS * HEAD_PAD), row(KV_LORA), row(MLA_ROPE),
                 row(BRANCH_W), row(BRANCH_W), row(BRANCH_W), row(BRANCH_W),
                 st(A_CONV - 1), st(D_CONV - 1)]
    if kv_tile:
        per = tm // kv_tile
        out_shape += [jax.ShapeDtypeStruct((B, T, MLA_HEADS * HEAD_PAD), BF16),
                      jax.ShapeDtypeStruct((B, T // kv_tile, MLA_HEADS * MLA_V, kv_tile), BF16)]
        out_specs += [row(MLA_HEADS * HEAD_PAD),
                      pl.BlockSpec((1, per, MLA_HEADS * MLA_V, kv_tile), lambda b, t: (b, t, 0, 0))]
    return pl.pallas_call(
        functools.partial(_proj_kernel, tm=tm, kv_tile=kv_tile),
        grid=(B, nt), in_specs=in_specs, out_specs=out_specs, out_shape=out_shape,
        scratch_shapes=[pltpu.VMEM((tm + 8, BRANCH_W), F32), pltpu.VMEM((tm + 40, BRANCH_W), F32)],
        compiler_params=_params(2), name="proj",
    )(x, gpre, w_small, conva, qn, wuq, kvn, cdw, cdb, lng, lnb, bufa, bufd, cq, sq, ck, sk,
      wk_pad, e_mat, wvt)


def _kvexpand_kernel(ckv_ref, kr_ref, wk_ref, e_ref, wvt_ref, k_ref, vt_ref):
    c = ckv_ref[0].astype(BF16)
    k_ref[0] = (_dot(c, wk_ref[...]) + _dot(kr_ref[0].astype(BF16), e_ref[...])).astype(BF16)
    vt_ref[0, 0] = lax.dot_general(wvt_ref[...], c, (((1,), (1,)), ((), ())),
                                   preferred_element_type=F32).astype(BF16)


def _kvexpand(ckv, kr, wk_pad, e_mat, wvt, tk):
    B, n, _ = ckv.shape
    nb = n // tk
    row = lambda w: pl.BlockSpec((1, tk, w), lambda b, j: (b, j, 0))
    return pl.pallas_call(
        _kvexpand_kernel, grid=(B, nb),
        in_specs=[row(KV_LORA), row(MLA_ROPE), _const_spec(wk_pad.shape), _const_spec(e_mat.shape),
                  _const_spec(wvt.shape)],
        out_specs=[row(MLA_HEADS * HEAD_PAD),
                   pl.BlockSpec((1, 1, MLA_HEADS * MLA_V, tk), lambda b, j: (b, j, 0, 0))],
        out_shape=[jax.ShapeDtypeStruct((B, n, MLA_HEADS * HEAD_PAD), BF16),
                   jax.ShapeDtypeStruct((B, nb, MLA_HEADS * MLA_V, tk), BF16)],
        compiler_params=_params(2), name="kvexpand",
    )(ckv, kr, wk_pad, e_mat, wvt)


def _attn_kernel(q_ref, k_ref, vt_ref, o_ref, qt_s, m_s, acc_s, *, causal, kv_valid, tq, tk):
    i = pl.program_id(1)
    strip = min(ATTN_STRIP, tq)
    qt_s[...] = q_ref[0].astype(F32).T.astype(BF16)
    m_s[...] = jnp.full(m_s.shape, -jnp.inf, F32)
    acc_s[...] = jnp.zeros(acc_s.shape, F32)
    ones_rows = jnp.ones((L_ROWS, tk), BF16)
    all_units = [(hh, c) for hh in range(MLA_HEADS) for c in range(tq // strip)]

    def block(j, mask, first_strip=0):
        units = [u for u in all_units if u[1] >= first_strip]
        r0 = j * tk
        if not isinstance(r0, int):
            r0 = pl.multiple_of(r0, tk)

        def scores(u):
            hh, c = u
            kh = k_ref[0, pl.ds(r0, tk), hh * HEAD_PAD:(hh + 1) * HEAD_PAD]
            return _dot(kh, qt_s[hh * HEAD_PAD:(hh + 1) * HEAD_PAD, c * strip:(c + 1) * strip])

        pend = [scores(u) for u in units[:ATTN_LOOKAHEAD]]
        for n, (hh, c) in enumerate(units):
            cs = slice(c * strip, (c + 1) * strip)
            s = pend.pop(0)
            if mask is not None:
                s = jnp.where(mask[:, cs], s, -jnp.inf)
            m_prev = m_s[hh:hh + 1, cs]
            m_new = jnp.maximum(m_prev, jnp.max(s, axis=0, keepdims=True))
            alpha = jnp.exp2(m_prev - m_new)
            p = jnp.exp2(s - m_new).astype(BF16)
            if n + ATTN_LOOKAHEAD < len(units):
                pend.append(scores(units[n + ATTN_LOOKAHEAD]))
            va = jnp.concatenate([vt_ref[0, j, hh * MLA_V:(hh + 1) * MLA_V, :], ones_rows], axis=0)
            acc_s[hh, :, cs] = alpha * acc_s[hh, :, cs] + _dot(va, p)
            m_s[hh:hh + 1, cs] = m_new

    if causal:
        ratio = tq // tk

        def full_block(j, carry):
            block(j, None)
            return carry

        lax.fori_loop(0, i * ratio, full_block, 0)
        qc = lax.broadcasted_iota(jnp.int32, (tk, tq), 1) // CHUNK
        for r in range(ratio):
            kc = (r * tk + lax.broadcasted_iota(jnp.int32, (tk, tq), 0)) // CHUNK
            block(i * ratio + r, kc <= qc, first_strip=(r * tk) // strip)
    else:
        nkv = k_ref.shape[1] // tk
        for j in range(nkv):
            if (j + 1) * tk > kv_valid:
                block(j, j * tk + lax.broadcasted_iota(jnp.int32, (tk, tq), 0) < kv_valid)
            else:
                block(j, None)

    outs = [acc_s[hh, :MLA_V, :] / acc_s[hh, MLA_V:MLA_V + 1, :] for hh in range(MLA_HEADS)]
    o_ref[0] = jnp.concatenate(outs, axis=0).T.astype(BF16)


def _attn(q, k, vt, causal, kv_valid, tq, tk):
    B, T, _ = q.shape
    nk = k.shape[1]
    return pl.pallas_call(
        functools.partial(_attn_kernel, causal=causal, kv_valid=kv_valid, tq=tq, tk=tk),
        grid=(B, T // tq),
        in_specs=[pl.BlockSpec((1, tq, MLA_HEADS * HEAD_PAD), lambda b, i: (b, i, 0)),
                  pl.BlockSpec((1, nk, MLA_HEADS * HEAD_PAD), lambda b, i: (b, 0, 0),
                               pipeline_mode=pl.Buffered(1)),
                  pl.BlockSpec((1, nk // tk, MLA_HEADS * MLA_V, tk), lambda b, i: (b, 0, 0, 0),
                               pipeline_mode=pl.Buffered(1))],
        out_specs=pl.BlockSpec((1, tq, MLA_HEADS * MLA_V), lambda b, i: (b, i, 0)),
        out_shape=jax.ShapeDtypeStruct((B, T, MLA_HEADS * MLA_V), BF16),
        scratch_shapes=[pltpu.VMEM((MLA_HEADS * HEAD_PAD, tq), BF16), pltpu.VMEM((MLA_HEADS, tq), F32),
                        pltpu.VMEM((MLA_HEADS, MLA_V + L_ROWS, tq), F32)],
        compiler_params=_params(2), name="attn",
    )(q, k, vt)


def _hgrn_kernel(hq_ref, hf_ref, hi_ref, hg_ref, lbraw_ref, gn_ref, s0_ref,
                 o_ref, sfin_ref, st_s, b2_s, c2_s, *, layer, tm, cc):
    t = pl.program_id(1)
    nt = pl.num_programs(1)
    W = HG_HEADS * HG_D

    @pl.when(t == 0)
    def _():
        for hh in range(HG_HEADS):
            st_s[hh] = s0_ref[0, hh].T

    raw = lbraw_ref[...]
    e = jnp.exp(raw - jnp.max(raw, axis=0, keepdims=True))
    pr = e / jnp.sum(e, axis=0, keepdims=True)
    cum = pr[0:1, :]
    for r in range(1, layer + 1):
        cum = cum + pr[r:r + 1, :]
    lb = cum - pr[0:1, :]
    log_lb = jnp.log(lb)
    log_1mlb = jnp.log1p(-lb)
    gn = gn_ref[...]

    tri = (lax.broadcasted_iota(jnp.int32, (cc, cc), 0)
           >= lax.broadcasted_iota(jnp.int32, (cc, cc), 1)).astype(BF16)
    lane_c = lax.broadcasted_iota(jnp.int32, (SUB, cc), 1)
    row_s = lax.broadcasted_iota(jnp.int32, (SUB, 1), 0)
    nsub = cc // SUB

    for c in range(tm // cc):
        rows = slice(c * cc, (c + 1) * cc)
        zf = hf_ref[0, rows, :]
        ls = jnp.minimum(zf, 0.0) - jnp.log(1.0 + jnp.exp(-jnp.abs(zf)))
        bb = log_1mlb + ls
        g = jnp.maximum(log_lb, bb) + jnp.log(1.0 + jnp.exp(-jnp.abs(log_lb - bb)))
        g1 = g.astype(BF16)
        r1 = g - g1.astype(F32)
        g2 = r1.astype(BF16)
        g3 = (r1 - g2.astype(F32)).astype(BF16)
        b2 = (_dot(tri, g1) + _dot(tri, g2) + _dot(tri, g3)) * LOG2E
        b2_s[rows, :] = b2
        c2_s[rows, :] = b2 - (bb - zf) * LOG2E

    def chunk(c, carry):
        r0 = pl.multiple_of(c * cc, cc)
        q = hq_ref[0, pl.ds(r0, cc), :].astype(F32)
        v = hi_ref[0, pl.ds(r0, cc), :]
        b2 = b2_s[pl.ds(r0, cc), :]
        c2 = c2_s[pl.ds(r0, cc), :]
        bl2 = b2[cc - 1:cc, :]
        qb = (q * jnp.exp2(b2)).astype(BF16)
        kdec = jnp.exp2(bl2 - c2).astype(BF16)
        ebl = jnp.exp2(bl2)

        o_inter = []
        for hh in range(HG_HEADS):
            sl = slice(hh * HG_D, (hh + 1) * HG_D)
            st = st_s[hh]
            o_inter.append(lax.dot_general(qb[:, sl], st.astype(BF16), (((1,), (1,)), ((), ())),
                                           preferred_element_type=F32))
            st_s[hh] = st * ebl[:, sl] + lax.dot_general(v[:, sl], kdec[:, sl], (((0,), (0,)), ((), ())),
                                                         preferred_element_type=F32)

        a_rows = []
        for si in range(nsub):
            lo = si * SUB
            q_i = q[lo:lo + SUB, :]
            b_i = b2[lo:lo + SUB, :]
            if si > 0:
                ref = b2[lo - 1:lo, :]
                qt = (q_i * jnp.exp2(b_i - ref)).astype(BF16)
                kt = jnp.concatenate([jnp.exp2(ref - c2[:lo, :]),
                                      jnp.zeros((cc - lo, W), F32)], axis=0).astype(BF16)
            a_h = []
            for hh in range(HG_HEADS):
                sl = slice(hh * HG_D, (hh + 1) * HG_D)
                if si > 0:
                    a = lax.dot_general(qt[:, sl], kt[:, sl], (((1,), (1,)), ((), ())),
                                        preferred_element_type=F32)
                else:
                    a = jnp.zeros((SUB, cc), F32)
                a_h.append(a)
            for s in range(SUB):
                z = q_i * jnp.exp2(b_i - c2[lo + s:lo + s + 1, :])
                place = (lane_c == lo + s) & (row_s >= s)
                for hh in range(HG_HEADS):
                    col = jnp.sum(z[:, hh * HG_D:(hh + 1) * HG_D], axis=-1, keepdims=True)
                    a_h[hh] = jnp.where(place, col, a_h[hh])
            a_rows.append(a_h)

        outs = []
        for hh in range(HG_HEADS):
            sl = slice(hh * HG_D, (hh + 1) * HG_D)
            a_full = jnp.concatenate([a_rows[si][hh] for si in range(nsub)], axis=0).astype(BF16)
            o = o_inter[hh] + _dot(a_full, v[:, sl])
            outs.append(o * lax.rsqrt(jnp.mean(o * o, axis=-1, keepdims=True) + EPS))
        on = jnp.concatenate(outs, axis=-1) * gn
        o_ref[0, pl.ds(r0, cc), :] = (on * _silu(hg_ref[0, pl.ds(r0, cc), :].astype(F32))).astype(BF16)
        return carry

    lax.fori_loop(0, tm // cc, chunk, 0)

    @pl.when(t == nt - 1)
    def _():
        for hh in range(HG_HEADS):
            sfin_ref[0, hh] = st_s[hh].T


def _hgrn(hq, hf, hi, hg, lb_raw, gn, s0, layer, tm, cc):
    B, T, W = hq.shape
    row = pl.BlockSpec((1, tm, W), lambda b, t: (b, t, 0))
    st = pl.BlockSpec((1, HG_HEADS, HG_D, HG_D), lambda b, t: (b, 0, 0, 0))
    return pl.pallas_call(
        functools.partial(_hgrn_kernel, layer=layer, tm=tm, cc=cc),
        grid=(B, T // tm),
        in_specs=[row, row, row, row, _const_spec(lb_raw.shape), _const_spec((1, W)), st],
        out_specs=[row, st],
        out_shape=[jax.ShapeDtypeStruct((B, T, W), BF16),
                   jax.ShapeDtypeStruct((B, HG_HEADS, HG_D, HG_D), F32)],
        scratch_shapes=[pltpu.VMEM((HG_HEADS, HG_D, HG_D), F32), pltpu.VMEM((tm, W), F32),
                        pltpu.VMEM((tm, W), F32)],
        compiler_params=_params(2), name="hgrn",
    )(hq, hf, hi, hg, lb_raw, gn, s0)


def _merge_kernel(x_ref, a_ref, b_ref, c_ref, d_ref, gpre_ref, wg_ref, bg_ref, wb_ref, wo_ref, gpost_ref,
                  y_ref):
    x = x_ref[...]
    h = _rms(x, gpre_ref[...]).astype(BF16)
    merged = None
    for i, br in enumerate((a_ref, b_ref, c_ref, d_ref)):
        gate = _sigmoid(_dot(h, wg_ref[:, i * D_MODEL:(i + 1) * D_MODEL])
                        + bg_ref[:, i * D_MODEL:(i + 1) * D_MODEL])
        term = gate * _dot(br[...], wb_ref[i])
        merged = term if merged is None else merged + term
    y = _dot(merged.astype(BF16), wo_ref[...])
    y_ref[...] = x + _rms(y, gpost_ref[...])


def _merge(x, br, gpre, wg, bg, wb, wo, gpost, tm):
    n = x.shape[0]
    row = lambda w: pl.BlockSpec((tm, w), lambda i: (i, 0))
    return pl.pallas_call(
        _merge_kernel, grid=(n // tm,),
        in_specs=[row(D_MODEL)] + [row(BRANCH_W)] * 4
                 + [_const_spec((1, D_MODEL)), _const_spec(wg.shape), _const_spec(bg.shape),
                    _const_spec(wb.shape), _const_spec(wo.shape), _const_spec((1, D_MODEL))],
        out_specs=row(D_MODEL),
        out_shape=jax.ShapeDtypeStruct((n, D_MODEL), F32),
        compiler_params=_params(1), name="merge",
    )(x, *br, gpre, wg, bg, wb, wo, gpost)


def _ffn_kernel(x_ref, gpre_ref, wup_ref, cw_ref, wdn_ref, gpost_ref, buf_ref,
                y_ref, nbuf_ref, g_s, *, tm, nsplit):
    t = pl.program_id(1)

    @pl.when(t == 0)
    def _():
        g_s[0:8, :] = jnp.zeros((8, D_FF), F32)
        g_s[8 - (FFN_CONV - 1):8, :] = buf_ref[0]

    x = x_ref[0]
    h = _rms(x, gpre_ref[...]).astype(BF16)
    wc = D_FF // nsplit
    ff = None
    for s in range(nsplit):
        lo = s * wc
        g = _dot(h, wup_ref[:, lo:lo + wc])
        g_s[8:8 + tm, lo:lo + wc] = g
        fg = (cw_ref[0:1, lo:lo + wc] * g_s[6:6 + tm, lo:lo + wc]
              + cw_ref[1:2, lo:lo + wc] * g_s[7:7 + tm, lo:lo + wc]
              + cw_ref[2:3, lo:lo + wc] * g)
        up = _dot(h, wup_ref[:, D_FF + lo:D_FF + lo + wc])
        part = _dot((_silu(fg) * up).astype(BF16), wdn_ref[lo:lo + wc, :])
        ff = part if ff is None else ff + part
    y_ref[0] = x + _rms(ff, gpost_ref[...])
    nbuf_ref[0] = g_s[tm + 6:tm + 8, :]
    g_s[0:8, :] = g_s[tm:tm + 8, :]


def _ffn(x, gpre, wup, cw, wdn, gpost, buf, tm):
    B, T, _ = x.shape
    row = pl.BlockSpec((1, tm, D_MODEL), lambda b, t: (b, t, 0))
    st = pl.BlockSpec((1, FFN_CONV - 1, D_FF), lambda b, t: (b, 0, 0))
    return pl.pallas_call(
        functools.partial(_ffn_kernel, tm=tm, nsplit=2),
        grid=(B, T // tm),
        in_specs=[row, _const_spec((1, D_MODEL)), _const_spec(wup.shape), _const_spec(cw.shape),
                  _const_spec(wdn.shape), _const_spec((1, D_MODEL)), st],
        out_specs=[row, st],
        out_shape=[jax.ShapeDtypeStruct((B, T, D_MODEL), F32),
                   jax.ShapeDtypeStruct((B, FFN_CONV - 1, D_FF), F32)],
        scratch_shapes=[pltpu.VMEM((tm + 8, D_FF), F32)],
        compiler_params=_params(2), name="ffn",
    )(x, gpre, wup, cw, wdn, gpost, buf)


def _tile(n, pref):
    return pref if n % pref == 0 else n


def _rope_tables(pos):
    half = MLA_ROPE // 2
    inv = ROPE_BASE ** (-jnp.arange(half, dtype=F32) / half)
    ang = pos.astype(F32)[:, None] * inv[None, :]
    cos, sin = jnp.cos(ang), jnp.sin(ang)
    n = pos.shape[0]
    scale = (MLA_NOPE + MLA_ROPE) ** -0.5 * np.log2(np.e)
    pad = jnp.zeros((n, HEAD_PAD - MLA_NOPE - MLA_ROPE), F32)
    cq = scale * jnp.concatenate([jnp.ones((n, MLA_NOPE), F32), cos, cos, pad], axis=1)
    sq = scale * jnp.concatenate([jnp.zeros((n, MLA_NOPE), F32), -sin, sin, pad], axis=1)
    ck = jnp.concatenate([cos, cos], axis=1)
    sk = jnp.concatenate([-sin, sin], axis=1)
    return cq, sq, ck, sk


def _layer_weights(l, norm_mix_pre, norm_mix_post, norm_ffn_pre, norm_ffn_post, w_in, b_gate, conv_a_w,
                   mla_q_norm, mla_w_uq, mla_kv_norm, mla_w_uk, mla_w_uv, hgrn_norm,
                   conv_d_w, conv_d_b, ln_d_g, ln_d_b, w_branch, w_out, ffn_w_up, ffn_conv_w, ffn_w_down):
    wi = w_in[l]
    o_kr = 3 * BRANCH_W + Q_LORA + KV_LORA
    o_h = o_kr + MLA_ROPE
    o_g = o_h + 6 * BRANCH_W
    half = MLA_ROPE // 2
    o_d = o_h + 4 * BRANCH_W
    n_dg = BRANCH_W // D_GROUP
    w_din = wi[:, o_d:o_g].reshape(D_MODEL, 2, n_dg, D_GROUP).transpose(0, 2, 1, 3).reshape(D_MODEL, 2 * BRANCH_W)
    w_small = jnp.concatenate(
        [wi[:, :o_kr], wi[:, o_h:o_d], w_din, wi[:, o_kr:o_h], wi[:, o_kr + half:o_h], wi[:, o_kr:o_kr + half]],
        axis=1).astype(BF16)
    w_gate = wi[:, o_g:].astype(BF16)
    uq = mla_w_uq[l].reshape(Q_LORA, MLA_HEADS, MLA_NOPE + MLA_ROPE)
    nope, r1, r2 = uq[..., :MLA_NOPE], uq[..., MLA_NOPE:MLA_NOPE + half], uq[..., MLA_NOPE + half:]
    z_pad = jnp.zeros((Q_LORA, MLA_HEADS, HEAD_PAD - MLA_NOPE - MLA_ROPE), F32)
    z_nope = jnp.zeros((Q_LORA, MLA_HEADS, MLA_NOPE), F32)
    wuq = jnp.concatenate(
        [jnp.concatenate([nope, r1, r2, z_pad], axis=-1).reshape(Q_LORA, MLA_HEADS * HEAD_PAD),
         jnp.concatenate([z_nope, r2, r1, z_pad], axis=-1).reshape(Q_LORA, MLA_HEADS * HEAD_PAD)],
        axis=1).astype(BF16)
    uk = mla_w_uk[l].reshape(KV_LORA, MLA_HEADS, MLA_NOPE)
    wk_pad = jnp.concatenate([uk, jnp.zeros((KV_LORA, MLA_HEADS, HEAD_PAD - MLA_NOPE), F32)],
                             axis=-1).reshape(KV_LORA, MLA_HEADS * HEAD_PAD).astype(BF16)
    e_np = np.zeros((MLA_ROPE, MLA_HEADS, HEAD_PAD), np.float32)
    for r in range(MLA_ROPE):
        e_np[r, :, MLA_NOPE + r] = 1.0
    e_mat = jnp.asarray(e_np.reshape(MLA_ROPE, MLA_HEADS * HEAD_PAD), BF16)
    r2d = lambda a: a[l].reshape(1, -1)
    return dict(
        gpre=r2d(norm_mix_pre), gpost=r2d(norm_mix_post), fpre=r2d(norm_ffn_pre), fpost=r2d(norm_ffn_post),
        w_small=w_small, w_gate=w_gate, b_gate=r2d(b_gate), conva=conv_a_w[l], qn=r2d(mla_q_norm), wuq=wuq,
        kvn=r2d(mla_kv_norm), wk_pad=wk_pad, e_mat=e_mat, wvt=mla_w_uv[l].T.astype(BF16), gn=r2d(hgrn_norm),
        cdw=conv_d_w[l], cdb=r2d(conv_d_b), lng=r2d(ln_d_g), lnb=r2d(ln_d_b),
        wb=w_branch[l].astype(BF16), wo=w_out[l].astype(BF16), wup=ffn_w_up[l].astype(BF16),
        cw=ffn_conv_w[l], wdn=ffn_w_down[l].astype(BF16))


def _layer(x, tabs, past, state, w, lb_raw, layer):
    B, T, _ = x.shape
    buf_a, s0, buf_d, buf_f = state
    prompt = past is None
    tp = _tile(T, TILE_PROJ)
    tk = _tile(tp, TILE_ATTN_K)
    outs = _proj(x, w["gpre"], w["w_small"], w["conva"], w["qn"], w["wuq"], w["kvn"], w["cdw"], w["cdb"],
                 w["lng"], w["lnb"], buf_a, buf_d, tabs, w["wk_pad"], w["e_mat"], w["wvt"],
                 tp, tk if prompt else 0)
    (out_a, out_d, q, ckv, kr, hq, hf, hi, hg, nbuf_a, nbuf_d) = outs[:11]

    if prompt:
        kcat, vt = outs[11:]
        out_b = _attn(q, kcat, vt, True, T, _tile(T, TILE_ATTN_Q), tk)
    else:
        n_keys = past[0].shape[1] + T
        n_pad = -n_keys % 128
        ckv_all = jnp.concatenate([past[0], ckv, jnp.zeros((B, n_pad, KV_LORA), F32)], axis=1)
        kr_all = jnp.concatenate([past[1], kr, jnp.zeros((B, n_pad, MLA_ROPE), F32)], axis=1)
        nk = n_keys + n_pad
        kcat, vt = _kvexpand(ckv_all, kr_all, w["wk_pad"], w["e_mat"], w["wvt"], nk)
        q_pad = -T % 128
        q_in = jnp.concatenate([q, jnp.zeros((B, q_pad, q.shape[-1]), q.dtype)], axis=1)
        out_b = _attn(q_in, kcat, vt, False, n_keys, T + q_pad, nk)[:, :T]

    cc = min(HG_CHUNK, T)
    out_c, s_new = _hgrn(hq, hf, hi, hg, lb_raw, w["gn"], s0, layer, _tile(T, TILE_HGRN), cc)

    flat = lambda a: a.reshape(B * T, a.shape[-1])
    x2 = _merge(flat(x), [flat(out_a), flat(out_b), flat(out_c), flat(out_d)], w["gpre"], w["w_gate"],
                w["b_gate"], w["wb"], w["wo"], w["gpost"], _tile(B * T, TILE_MERGE))
    x3, nbuf_f = _ffn(x2.reshape(B, T, D_MODEL), w["fpre"], w["wup"], w["cw"], w["wdn"], w["fpost"], buf_f,
                      _tile(T, TILE_FFN))
    return x3, (ckv, kr, nbuf_a, s_new, nbuf_d, nbuf_f)


def kernel(x_prompt, x_sample, cache_ckv, cache_krope, state_conv_a, state_hgrn, state_conv_d, state_ffn_conv,
           norm_mix_pre, norm_mix_post, norm_ffn_pre, norm_ffn_post, w_in, b_gate, conv_a_w,
           mla_q_norm, mla_w_uq, mla_kv_norm, mla_w_uk, mla_w_uv, hgrn_lb_raw, hgrn_norm,
           conv_d_w, conv_d_b, ln_d_g, ln_d_b, w_branch, w_out, ffn_w_up, ffn_conv_w, ffn_w_down):
    depth = w_in.shape[0]
    B, T, _ = x_prompt.shape
    Bs, Ts, _ = x_sample.shape
    past_len = cache_ckv.shape[2]
    tabs_p = _rope_tables(jnp.arange(T))
    tabs_s = _rope_tables(past_len + jnp.arange(Ts))
    zero_state = (jnp.zeros((B, A_CONV - 1, BRANCH_W), F32), jnp.zeros((B, HG_HEADS, HG_D, HG_D), F32),
                  jnp.zeros((B, D_CONV - 1, BRANCH_W), F32), jnp.zeros((B, FFN_CONV - 1, D_FF), F32))
    lb_raw = hgrn_lb_raw.astype(F32)
    yp, ys = x_prompt, x_sample
    p_states, s_states = [], []
    for l in range(depth):
        w = _layer_weights(l, norm_mix_pre, norm_mix_post, norm_ffn_pre, norm_ffn_post, w_in, b_gate, conv_a_w,
                           mla_q_norm, mla_w_uq, mla_kv_norm, mla_w_uk, mla_w_uv, hgrn_norm,
                           conv_d_w, conv_d_b, ln_d_g, ln_d_b, w_branch, w_out, ffn_w_up, ffn_conv_w,
                           ffn_w_down)
        yp, sp = _layer(yp, tabs_p, None, zero_state, w, lb_raw, l)
        ys, ss = _layer(ys, tabs_s, (cache_ckv[l], cache_krope[l]),
                        (state_conv_a[l], state_hgrn[l], state_conv_d[l], state_ffn_conv[l]), w, lb_raw, l)
        p_states.append(sp)
        s_states.append(ss)
    ps = [jnp.stack(z) for z in zip(*p_states)]
    sst = [jnp.stack(z) for z in zip(*s_states)]
    return (yp, ys, ps[0], ps[1], ps[2], ps[3], ps[4], ps[5], sst[0], sst[1], sst[2], sst[3], sst[4], sst[5])
```

```python
import functools

import numpy as np
import jax
import jax.numpy as jnp
from jax import lax
from jax.experimental import pallas as pl
from jax.experimental.pallas import tpu as pltpu

D_MODEL = 1024
BRANCH_W = 512
N_BRANCH = 4
MLA_HEADS = 8
MLA_NOPE = 64
MLA_ROPE = 32
MLA_V = 64
Q_LORA = 384
KV_LORA = 256
ROPE_BASE = 10000.0
HG_HEADS = 4
HG_D = 128
A_CONV = 3
D_CONV = 31
FFN_CONV = 3
D_FF = 2816
CHUNK = 64
EPS = 1e-6
LOG2E = 1.4426950408889634
HEAD_PAD = 128
HG_CHUNK = 128
SUB = 8
D_GROUP = 128

_C_A = 0
_C_CQ = 3 * BRANCH_W
_C_CKV = _C_CQ + Q_LORA
_C_HG = _C_CKV + KV_LORA
_C_DIN = _C_HG + 4 * BRANCH_W
_C_KR = _C_DIN + 2 * BRANCH_W
_C_END = _C_KR + 2 * MLA_ROPE

TILE_PROJ = 512
TILE_ATTN_Q = 1024
TILE_ATTN_K = 256
ATTN_STRIP = 256
ATTN_LOOKAHEAD = 6
L_ROWS = 16
TILE_HGRN = 512
TILE_MERGE = 512
TILE_FFN = 512
VMEM_LIMIT = 56 * 1024 * 1024

F32 = jnp.float32
BF16 = jnp.bfloat16


def _rms(x, g):
    return x * lax.rsqrt(jnp.mean(x * x, axis=-1, keepdims=True) + EPS) * g


def _sigmoid(x):
    return 1.0 / (1.0 + jnp.exp(-x))


def _silu(x):
    return x * _sigmoid(x)


def _dot(a, b):
    return jnp.dot(a, b, preferred_element_type=F32)


def _const_spec(shape):
    nd = len(shape)
    return pl.BlockSpec(shape, lambda *_: (0,) * nd, pipeline_mode=pl.Buffered(1))


def _params(n_axes):
    return pltpu.CompilerParams(dimension_semantics=("arbitrary",) * n_axes,
                                vmem_limit_bytes=VMEM_LIMIT)


def _proj_kernel(x_ref, gpre_ref, w_ref, conva_ref, qn_ref, wuq_ref, kvn_ref,
                 cdw_ref, cdb_ref, lng_ref, lnb_ref, bufa_ref, bufd_ref,
                 cq_ref, sq_ref, ck_ref, sk_ref, wk_ref, e_ref, wvt_ref,
                 outa_ref, outd_ref, q_ref, ckv_ref, kr_ref,
                 hq_ref, hf_ref, hi_ref, hg_ref, nbufa_ref, nbufd_ref,
                 *rest, tm, kv_tile):
    if kv_tile:
        k_ref, vt_ref, ua_s, gd_s = rest
    else:
        ua_s, gd_s = rest
    t = pl.program_id(1)

    @pl.when(t == 0)
    def _():
        ua_s[0:8, :] = jnp.zeros((8, BRANCH_W), F32)
        ua_s[8 - (A_CONV - 1):8, :] = bufa_ref[0]
        gd_s[0:32, :] = jnp.zeros((32, BRANCH_W), F32)
        gd_s[32 - (D_CONV - 1):32, :] = bufd_ref[0]
        gd_s[tm + 32:tm + 40, :] = jnp.zeros((8, BRANCH_W), F32)

    h = _rms(x_ref[0], gpre_ref[...]).astype(BF16)

    base = 32 - (D_CONV - 1)

    def glu_group(g):
        pd = _dot(h, w_ref[:, _C_DIN + 2 * D_GROUP * g:_C_DIN + 2 * D_GROUP * (g + 1)])
        gd_s[32:32 + tm, D_GROUP * g:D_GROUP * (g + 1)] = pd[:, :D_GROUP] * _sigmoid(pd[:, D_GROUP:])

    def conv_group(g):
        cs = slice(D_GROUP * g, D_GROUP * (g + 1))
        y = cdb_ref[:, cs]
        for r in range(8):
            part = None
            for k in [k for k in range(D_CONV) if (base + k) % 8 == r]:
                lo = base + k - r
                term = cdw_ref[k:k + 1, cs] * gd_s[lo:lo + tm + 8, cs]
                part = term if part is None else part + term
            y = y + part[r:r + tm, :]
        return y

    glu_group(0)
    a_b = _dot(h, w_ref[:, _C_A:_C_A + BRANCH_W])
    yd0 = conv_group(0)
    a_c = _dot(h, w_ref[:, _C_A + BRANCH_W:_C_A + 2 * BRANCH_W])
    glu_group(1)
    a_x = _dot(h, w_ref[:, _C_A + 2 * BRANCH_W:_C_CQ])
    u = a_c * a_x
    ua_s[8:8 + tm, :] = u
    ya = (conva_ref[0:1, :] * ua_s[6:6 + tm, :] + conva_ref[1:2, :] * ua_s[7:7 + tm, :]
          + conva_ref[2:3, :] * u)
    outa_ref[0] = (a_b * ya).astype(BF16)
    nbufa_ref[0] = ua_s[tm + 6:tm + 8, :]
    ua_s[0:8, :] = ua_s[tm:tm + 8, :]
    yd1 = conv_group(1)

    pq = _dot(h, w_ref[:, _C_CQ:_C_HG])
    cqn = _rms(pq[:, :Q_LORA], qn_ref[...]).astype(BF16)
    ckv = _rms(pq[:, Q_LORA:], kvn_ref[...])
    ckv_ref[0] = ckv
    glu_group(2)
    qab = _dot(cqn, wuq_ref[...])
    cq = cq_ref[...]
    sq = sq_ref[...]
    nq = MLA_HEADS * HEAD_PAD
    for hh in range(MLA_HEADS):
        lo = hh * HEAD_PAD
        q_ref[0, :, lo:lo + HEAD_PAD] = (qab[:, lo:lo + HEAD_PAD] * cq
                                         + qab[:, nq + lo:nq + lo + HEAD_PAD] * sq).astype(BF16)
    yd2 = conv_group(2)
    pk = _dot(h, w_ref[:, _C_KR:_C_END])
    kr = pk[:, :MLA_ROPE] * ck_ref[...] + pk[:, MLA_ROPE:] * sk_ref[...]
    kr_ref[0] = kr
    if kv_tile:
        c = ckv.astype(BF16)
        k_ref[0] = (_dot(c, wk_ref[...]) + _dot(kr.astype(BF16), e_ref[...])).astype(BF16)
        vt = lax.dot_general(wvt_ref[...], c, (((1,), (1,)), ((), ())),
                             preferred_element_type=F32).astype(BF16)
        for i in range(tm // kv_tile):
            vt_ref[0, i] = vt[:, i * kv_tile:(i + 1) * kv_tile]

    hq_ref[0] = _dot(h, w_ref[:, _C_HG:_C_HG + BRANCH_W]).astype(BF16)
    glu_group(3)
    hf_ref[0] = _dot(h, w_ref[:, _C_HG + BRANCH_W:_C_HG + 2 * BRANCH_W])
    yd3 = conv_group(3)
    hi_ref[0] = _dot(h, w_ref[:, _C_HG + 2 * BRANCH_W:_C_HG + 3 * BRANCH_W]).astype(BF16)
    hg_ref[0] = _dot(h, w_ref[:, _C_HG + 3 * BRANCH_W:_C_DIN]).astype(BF16)

    yd = jnp.concatenate([yd0, yd1, yd2, yd3], axis=-1)
    mu = jnp.mean(yd, axis=-1, keepdims=True)
    yc = yd - mu
    ln = yc * lax.rsqrt(jnp.mean(yc * yc, axis=-1, keepdims=True) + EPS) * lng_ref[...] + lnb_ref[...]
    outd_ref[0] = _silu(ln).astype(BF16)
    nbufd_ref[0] = gd_s[tm + 2:tm + 32, :]
    gd_s[0:32, :] = gd_s[tm:tm + 32, :]


def _proj(x, gpre, w_small, conva, qn, wuq, kvn, cdw, cdb, lng, lnb, bufa, bufd, tabs, wk_pad, e_mat, wvt,
          tm, kv_tile):
    B, T, _ = x.shape
    cq, sq, ck, sk = tabs
    nt = T // tm
    row = lambda w: pl.BlockSpec((1, tm, w), lambda b, t: (b, t, 0))
    tab = lambda w: pl.BlockSpec((tm, w), lambda b, t: (t, 0))
    st = lambda r: pl.BlockSpec((1, r, BRANCH_W), lambda b, t: (b, 0, 0))
    in_specs = [row(D_MODEL), _const_spec((1, D_MODEL)), _const_spec(w_small.shape),
                _const_spec(conva.shape), _const_spec((1, Q_LORA)), _const_spec(wuq.shape),
                _const_spec((1, KV_LORA)), _const_spec(cdw.shape), _const_spec((1, BRANCH_W)),
                _const_spec((1, BRANCH_W)), _const_spec((1, BRANCH_W)),
                st(A_CONV - 1), st(D_CONV - 1),
                tab(HEAD_PAD), tab(HEAD_PAD), tab(MLA_ROPE), tab(MLA_ROPE),
                _const_spec(wk_pad.shape), _const_spec(e_mat.shape), _const_spec(wvt.shape)]
    out_shape = [jax.ShapeDtypeStruct((B, T, BRANCH_W), BF16),
                 jax.ShapeDtypeStruct((B, T, BRANCH_W), BF16),
                 jax.ShapeDtypeStruct((B, T, MLA_HEADS * HEAD_PAD), BF16),
                 jax.ShapeDtypeStruct((B, T, KV_LORA), F32),
                 jax.ShapeDtypeStruct((B, T, MLA_ROPE), F32),
                 jax.ShapeDtypeStruct((B, T, BRANCH_W), BF16),
                 jax.ShapeDtypeStruct((B, T, BRANCH_W), F32),
                 jax.ShapeDtypeStruct((B, T, BRANCH_W), BF16),
                 jax.ShapeDtypeStruct((B, T, BRANCH_W), BF16),
                 jax.ShapeDtypeStruct((B, A_CONV - 1, BRANCH_W), F32),
                 jax.ShapeDtypeStruct((B, D_CONV - 1, BRANCH_W), F32)]
    out_specs = [row(BRANCH_W), row(BRANCH_W), row(MLA_HEADS * HEAD_PAD), row(KV_LORA), row(MLA_ROPE),
                 row(BRANCH_W), row(BRANCH_W), row(BRANCH_W), row(BRANCH_W),
                 st(A_CONV - 1), st(D_CONV - 1)]
    if kv_tile:
        per = tm // kv_tile
        out_shape += [jax.ShapeDtypeStruct((B, T, MLA_HEADS * HEAD_PAD), BF16),
                      jax.ShapeDtypeStruct((B, T // kv_tile, MLA_HEADS * MLA_V, kv_tile), BF16)]
        out_specs += [row(MLA_HEADS * HEAD_PAD),
                      pl.BlockSpec((1, per, MLA_HEADS * MLA_V, kv_tile), lambda b, t: (b, t, 0, 0))]
    return pl.pallas_call(
        functools.partial(_proj_kernel, tm=tm, kv_tile=kv_tile),
        grid=(B, nt), in_specs=in_specs, out_specs=out_specs, out_shape=out_shape,
        scratch_shapes=[pltpu.VMEM((tm + 8, BRANCH_W), F32), pltpu.VMEM((tm + 40, BRANCH_W), F32)],
        compiler_params=_params(2), name="proj",
    )(x, gpre, w_small, conva, qn, wuq, kvn, cdw, cdb, lng, lnb, bufa, bufd, cq, sq, ck, sk,
      wk_pad, e_mat, wvt)


def _kvexpand_kernel(ckv_ref, kr_ref, wk_ref, e_ref, wvt_ref, k_ref, vt_ref):
    c = ckv_ref[0].astype(BF16)
    k_ref[0] = (_dot(c, wk_ref[...]) + _dot(kr_ref[0].astype(BF16), e_ref[...])).astype(BF16)
    vt_ref[0, 0] = lax.dot_general(wvt_ref[...], c, (((1,), (1,)), ((), ())),
                                   preferred_element_type=F32).astype(BF16)


def _kvexpand(ckv, kr, wk_pad, e_mat, wvt, tk):
    B, n, _ = ckv.shape
    nb = n // tk
    row = lambda w: pl.BlockSpec((1, tk, w), lambda b, j: (b, j, 0))
    return pl.pallas_call(
        _kvexpand_kernel, grid=(B, nb),
        in_specs=[row(KV_LORA), row(MLA_ROPE), _const_spec(wk_pad.shape), _const_spec(e_mat.shape),
                  _const_spec(wvt.shape)],
        out_specs=[row(MLA_HEADS * HEAD_PAD),
                   pl.BlockSpec((1, 1, MLA_HEADS * MLA_V, tk), lambda b, j: (b, j, 0, 0))],
        out_shape=[jax.ShapeDtypeStruct((B, n, MLA_HEADS * HEAD_PAD), BF16),
                   jax.ShapeDtypeStruct((B, nb, MLA_HEADS * MLA_V, tk), BF16)],
        compiler_params=_params(2), name="kvexpand",
    )(ckv, kr, wk_pad, e_mat, wvt)


def _attn_kernel(q_ref, k_ref, vt_ref, o_ref, qt_s, m_s, acc_s, *, causal, kv_valid, tq, tk):
    i = pl.program_id(1)
    strip = min(ATTN_STRIP, tq)
    qt_s[...] = q_ref[0].astype(F32).T.astype(BF16)
    m_s[...] = jnp.full(m_s.shape, -jnp.inf, F32)
    acc_s[...] = jnp.zeros(acc_s.shape, F32)
    ones_rows = jnp.ones((L_ROWS, tk), BF16)
    all_units = [(hh, c) for hh in range(MLA_HEADS) for c in range(tq // strip)]

    def block(j, mask, first_strip=0):
        units = [u for u in all_units if u[1] >= first_strip]
        r0 = j * tk
        if not isinstance(r0, int):
            r0 = pl.multiple_of(r0, tk)

        def scores(u):
            hh, c = u
            kh = k_ref[0, pl.ds(r0, tk), hh * HEAD_PAD:(hh + 1) * HEAD_PAD]
            return _dot(kh, qt_s[hh * HEAD_PAD:(hh + 1) * HEAD_PAD, c * strip:(c + 1) * strip])

        pend = [scores(u) for u in units[:ATTN_LOOKAHEAD]]
        for n, (hh, c) in enumerate(units):
            cs = slice(c * strip, (c + 1) * strip)
            s = pend.pop(0)
            if mask is not None:
                s = jnp.where(mask[:, cs], s, -jnp.inf)
            m_prev = m_s[hh:hh + 1, cs]
            m_new = jnp.maximum(m_prev, jnp.max(s, axis=0, keepdims=True))
            alpha = jnp.exp2(m_prev - m_new)
            p = jnp.exp2(s - m_new).astype(BF16)
            if n + ATTN_LOOKAHEAD < len(units):
                pend.append(scores(units[n + ATTN_LOOKAHEAD]))
            va = jnp.concatenate([vt_ref[0, j, hh * MLA_V:(hh + 1) * MLA_V, :], ones_rows], axis=0)
            acc_s[hh, :, cs] = alpha * acc_s[hh, :, cs] + _dot(va, p)
            m_s[hh:hh + 1, cs] = m_new

    if causal:
        ratio = tq // tk

        def full_block(j, carry):
            block(j, None)
            return carry

        lax.fori_loop(0, i * ratio, full_block, 0)
        qc = lax.broadcasted_iota(jnp.int32, (tk, tq), 1) // CHUNK
        for r in range(ratio):
            kc = (r * tk + lax.broadcasted_iota(jnp.int32, (tk, tq), 0)) // CHUNK
            block(i * ratio + r, kc <= qc, first_strip=(r * tk) // strip)
    else:
        nkv = k_ref.shape[1] // tk
        for j in range(nkv):
            if (j + 1) * tk > kv_valid:
                block(j, j * tk + lax.broadcasted_iota(jnp.int32, (tk, tq), 0) < kv_valid)
            else:
                block(j, None)

    outs = [acc_s[hh, :MLA_V, :] / acc_s[hh, MLA_V:MLA_V + 1, :] for hh in range(MLA_HEADS)]
    o_ref[0] = jnp.concatenate(outs, axis=0).T.astype(BF16)


def _attn(q, k, vt, causal, kv_valid, tq, tk):
    B, T, _ = q.shape
    nk = k.shape[1]
    return pl.pallas_call(
        functools.partial(_attn_kernel, causal=causal, kv_valid=kv_valid, tq=tq, tk=tk),
        grid=(B, T // tq),
        in_specs=[pl.BlockSpec((1, tq, MLA_HEADS * HEAD_PAD), lambda b, i: (b, i, 0)),
                  pl.BlockSpec((1, nk, MLA_HEADS * HEAD_PAD), lambda b, i: (b, 0, 0),
                               pipeline_mode=pl.Buffered(1)),
                  pl.BlockSpec((1, nk // tk, MLA_HEADS * MLA_V, tk), lambda b, i: (b, 0, 0, 0),
                               pipeline_mode=pl.Buffered(1))],
        out_specs=pl.BlockSpec((1, tq, MLA_HEADS * MLA_V), lambda b, i: (b, i, 0)),
        out_shape=jax.ShapeDtypeStruct((B, T, MLA_HEADS * MLA_V), BF16),
        scratch_shapes=[pltpu.VMEM((MLA_HEADS * HEAD_PAD, tq), BF16), pltpu.VMEM((MLA_HEADS, tq), F32),
                        pltpu.VMEM((MLA_HEADS, MLA_V + L_ROWS, tq), F32)],
        compiler_params=_params(2), name="attn",
    )(q, k, vt)


def _hgrn_kernel(hq_ref, hf_ref, hi_ref, hg_ref, lbraw_ref, gn_ref, s0_ref,
                 o_ref, sfin_ref, st_s, b2_s, c2_s, *, layer, tm, cc):
    t = pl.program_id(1)
    nt = pl.num_programs(1)
    W = HG_HEADS * HG_D

    @pl.when(t == 0)
    def _():
        for hh in range(HG_HEADS):
            st_s[hh] = s0_ref[0, hh].T

    raw = lbraw_ref[...]
    e = jnp.exp(raw - jnp.max(raw, axis=0, keepdims=True))
    pr = e / jnp.sum(e, axis=0, keepdims=True)
    cum = pr[0:1, :]
    for r in range(1, layer + 1):
        cum = cum + pr[r:r + 1, :]
    lb = cum - pr[0:1, :]
    log_lb = jnp.log(lb)
    log_1mlb = jnp.log1p(-lb)
    gn = gn_ref[...]

    tri = (lax.broadcasted_iota(jnp.int32, (cc, cc), 0)
           >= lax.broadcasted_iota(jnp.int32, (cc, cc), 1)).astype(BF16)
    lane_c = lax.broadcasted_iota(jnp.int32, (SUB, cc), 1)
    row_s = lax.broadcasted_iota(jnp.int32, (SUB, 1), 0)
    nsub = cc // SUB

    for c in range(tm // cc):
        rows = slice(c * cc, (c + 1) * cc)
        zf = hf_ref[0, rows, :]
        ls = jnp.minimum(zf, 0.0) - jnp.log(1.0 + jnp.exp(-jnp.abs(zf)))
        bb = log_1mlb + ls
        g = jnp.maximum(log_lb, bb) + jnp.log(1.0 + jnp.exp(-jnp.abs(log_lb - bb)))
        g1 = g.astype(BF16)
        r1 = g - g1.astype(F32)
        g2 = r1.astype(BF16)
        g3 = (r1 - g2.astype(F32)).astype(BF16)
        b2 = (_dot(tri, g1) + _dot(tri, g2) + _dot(tri, g3)) * LOG2E
        b2_s[rows, :] = b2
        c2_s[rows, :] = b2 - (bb - zf) * LOG2E

    def chunk(c, carry):
        r0 = pl.multiple_of(c * cc, cc)
        q = hq_ref[0, pl.ds(r0, cc), :].astype(F32)
        v = hi_ref[0, pl.ds(r0, cc), :]
        b2 = b2_s[pl.ds(r0, cc), :]
        c2 = c2_s[pl.ds(r0, cc), :]
        bl2 = b2[cc - 1:cc, :]
        qb = (q * jnp.exp2(b2)).astype(BF16)
        kdec = jnp.exp2(bl2 - c2).astype(BF16)
        ebl = jnp.exp2(bl2)

        o_inter = []
        for hh in range(HG_HEADS):
            sl = slice(hh * HG_D, (hh + 1) * HG_D)
            st = st_s[hh]
            o_inter.append(lax.dot_general(qb[:, sl], st.astype(BF16), (((1,), (1,)), ((), ())),
                                           preferred_element_type=F32))
            st_s[hh] = st * ebl[:, sl] + lax.dot_general(v[:, sl], kdec[:, sl], (((0,), (0,)), ((), ())),
                                                         preferred_element_type=F32)

        a_rows = []
        for si in range(nsub):
            lo = si * SUB
            q_i = q[lo:lo + SUB, :]
            b_i = b2[lo:lo + SUB, :]
            if si > 0:
                ref = b2[lo - 1:lo, :]
                qt = (q_i * jnp.exp2(b_i - ref)).astype(BF16)
                kt = jnp.concatenate([jnp.exp2(ref - c2[:lo, :]),
                                      jnp.zeros((cc - lo, W), F32)], axis=0).astype(BF16)
            a_h = []
            for hh in range(HG_HEADS):
                sl = slice(hh * HG_D, (hh + 1) * HG_D)
                if si > 0:
                    a = lax.dot_general(qt[:, sl], kt[:, sl], (((1,), (1,)), ((), ())),
                                        preferred_element_type=F32)
                else:
                    a = jnp.zeros((SUB, cc), F32)
                a_h.append(a)
            for s in range(SUB):
                z = q_i * jnp.exp2(b_i - c2[lo + s:lo + s + 1, :])
                place = (lane_c == lo + s) & (row_s >= s)
                for hh in range(HG_HEADS):
                    col = jnp.sum(z[:, hh * HG_D:(hh + 1) * HG_D], axis=-1, keepdims=True)
                    a_h[hh] = jnp.where(place, col, a_h[hh])
            a_rows.append(a_h)

        outs = []
        for hh in range(HG_HEADS):
            sl = slice(hh * HG_D, (hh + 1) * HG_D)
            a_full = jnp.concatenate([a_rows[si][hh] for si in range(nsub)], axis=0).astype(BF16)
            o = o_inter[hh] + _dot(a_full, v[:, sl])
            outs.append(o * lax.rsqrt(jnp.mean(o * o, axis=-1, keepdims=True) + EPS))
        on = jnp.concatenate(outs, axis=-1) * gn
        o_ref[0, pl.ds(r0, cc), :] = (on * _silu(hg_ref[0, pl.ds(r0, cc), :].astype(F32))).astype(BF16)
        return carry

    lax.fori_loop(0, tm // cc, chunk, 0)

    @pl.when(t == nt - 1)
    def _():
        for hh in range(HG_HEADS):
            sfin_ref[0, hh] = st_s[hh].T


def _hgrn(hq, hf, hi, hg, lb_raw, gn, s0, layer, tm, cc):
    B, T, W = hq.shape
    row = pl.BlockSpec((1, tm, W), lambda b, t: (b, t, 0))
    st = pl.BlockSpec((1, HG_HEADS, HG_D, HG_D), lambda b, t: (b, 0, 0, 0))
    return pl.pallas_call(
        functools.partial(_hgrn_kernel, layer=layer, tm=tm, cc=cc),
        grid=(B, T // tm),
        in_specs=[row, row, row, row, _const_spec(lb_raw.shape), _const_spec((1, W)), st],
        out_specs=[row, st],
        out_shape=[jax.ShapeDtypeStruct((B, T, W), BF16),
                   jax.ShapeDtypeStruct((B, HG_HEADS, HG_D, HG_D), F32)],
        scratch_shapes=[pltpu.VMEM((HG_HEADS, HG_D, HG_D), F32), pltpu.VMEM((tm, W), F32),
                        pltpu.VMEM((tm, W), F32)],
        compiler_params=_params(2), name="hgrn",
    )(hq, hf, hi, hg, lb_raw, gn, s0)


def _merge_kernel(x_ref, a_ref, b_ref, c_ref, d_ref, gpre_ref, wg_ref, bg_ref, wb_ref, wo_ref, gpost_ref,
                  y_ref):
    x = x_ref[...]
    h = _rms(x, gpre_ref[...]).astype(BF16)
    merged = None
    for i, br in enumerate((a_ref, b_ref, c_ref, d_ref)):
        gate = _sigmoid(_dot(h, wg_ref[:, i * D_MODEL:(i + 1) * D_MODEL])
                        + bg_ref[:, i * D_MODEL:(i + 1) * D_MODEL])
        term = gate * _dot(br[...], wb_ref[i])
        merged = term if merged is None else merged + term
    y = _dot(merged.astype(BF16), wo_ref[...])
    y_ref[...] = x + _rms(y, gpost_ref[...])


def _merge(x, br, gpre, wg, bg, wb, wo, gpost, tm):
    n = x.shape[0]
    row = lambda w: pl.BlockSpec((tm, w), lambda i: (i, 0))
    return pl.pallas_call(
        _merge_kernel, grid=(n // tm,),
        in_specs=[row(D_MODEL)] + [row(BRANCH_W)] * 4
                 + [_const_spec((1, D_MODEL)), _const_spec(wg.shape), _const_spec(bg.shape),
                    _const_spec(wb.shape), _const_spec(wo.shape), _const_spec((1, D_MODEL))],
        out_specs=row(D_MODEL),
        out_shape=jax.ShapeDtypeStruct((n, D_MODEL), F32),
        compiler_params=_params(1), name="merge",
    )(x, *br, gpre, wg, bg, wb, wo, gpost)


def _ffn_kernel(x_ref, gpre_ref, wup_ref, cw_ref, wdn_ref, gpost_ref, buf_ref,
                y_ref, nbuf_ref, g_s, *, tm, nsplit):
    t = pl.program_id(1)

    @pl.when(t == 0)
    def _():
        g_s[0:8, :] = jnp.zeros((8, D_FF), F32)
        g_s[8 - (FFN_CONV - 1):8, :] = buf_ref[0]

    x = x_ref[0]
    h = _rms(x, gpre_ref[...]).astype(BF16)
    wc = D_FF // nsplit
    ff = None
    for s in range(nsplit):
        lo = s * wc
        g = _dot(h, wup_ref[:, lo:lo + wc])
        g_s[8:8 + tm, lo:lo + wc] = g
        fg = (cw_ref[0:1, lo:lo + wc] * g_s[6:6 + tm, lo:lo + wc]
              + cw_ref[1:2, lo:lo + wc] * g_s[7:7 + tm, lo:lo + wc]
              + cw_ref[2:3, lo:lo + wc] * g)
        up = _dot(h, wup_ref[:, D_FF + lo:D_FF + lo + wc])
        part = _dot((_silu(fg) * up).astype(BF16), wdn_ref[lo:lo + wc, :])
        ff = part if ff is None else ff + part
    y_ref[0] = x + _rms(ff, gpost_ref[...])
    nbuf_ref[0] = g_s[tm + 6:tm + 8, :]
    g_s[0:8, :] = g_s[tm:tm + 8, :]


def _ffn(x, gpre, wup, cw, wdn, gpost, buf, tm):
    B, T, _ = x.shape
    row = pl.BlockSpec((1, tm, D_MODEL), lambda b, t: (b, t, 0))
    st = pl.BlockSpec((1, FFN_CONV - 1, D_FF), lambda b, t: (b, 0, 0))
    return pl.pallas_call(
        functools.partial(_ffn_kernel, tm=tm, nsplit=1),
        grid=(B, T // tm),
        in_specs=[row, _const_spec((1, D_MODEL)), _const_spec(wup.shape), _const_spec(cw.shape),
                  _const_spec(wdn.shape), _const_spec((1, D_MODEL)), st],
        out_specs=[row, st],
        out_shape=[jax.ShapeDtypeStruct((B, T, D_MODEL), F32),
                   jax.ShapeDtypeStruct((B, FFN_CONV - 1, D_FF), F32)],
        scratch_shapes=[pltpu.VMEM((tm + 8, D_FF), F32)],
        compiler_params=_params(2), name="ffn",
    )(x, gpre, wup, cw, wdn, gpost, buf)


def _tile(n, pref):
    return pref if n % pref == 0 else n


def _rope_tables(pos):
    half = MLA_ROPE // 2
    inv = ROPE_BASE ** (-jnp.arange(half, dtype=F32) / half)
    ang = pos.astype(F32)[:, None] * inv[None, :]
    cos, sin = jnp.cos(ang), jnp.sin(ang)
    n = pos.shape[0]
    scale = (MLA_NOPE + MLA_ROPE) ** -0.5 * np.log2(np.e)
    pad = jnp.zeros((n, HEAD_PAD - MLA_NOPE - MLA_ROPE), F32)
    cq = scale * jnp.concatenate([jnp.ones((n, MLA_NOPE), F32), cos, cos, pad], axis=1)
    sq = scale * jnp.concatenate([jnp.zeros((n, MLA_NOPE), F32), -sin, sin, pad], axis=1)
    ck = jnp.concatenate([cos, cos], axis=1)
    sk = jnp.concatenate([-sin, sin], axis=1)
    return cq, sq, ck, sk


def _layer_weights(l, norm_mix_pre, norm_mix_post, norm_ffn_pre, norm_ffn_post, w_in, b_gate, conv_a_w,
                   mla_q_norm, mla_w_uq, mla_kv_norm, mla_w_uk, mla_w_uv, hgrn_norm,
                   conv_d_w, conv_d_b, ln_d_g, ln_d_b, w_branch, w_out, ffn_w_up, ffn_conv_w, ffn_w_down):
    wi = w_in[l]
    o_kr = 3 * BRANCH_W + Q_LORA + KV_LORA
    o_h = o_kr + MLA_ROPE
    o_g = o_h + 6 * BRANCH_W
    half = MLA_ROPE // 2
    o_d = o_h + 4 * BRANCH_W
    n_dg = BRANCH_W // D_GROUP
    w_din = wi[:, o_d:o_g].reshape(D_MODEL, 2, n_dg, D_GROUP).transpose(0, 2, 1, 3).reshape(D_MODEL, 2 * BRANCH_W)
    w_small = jnp.concatenate(
        [wi[:, :o_kr], wi[:, o_h:o_d], w_din, wi[:, o_kr:o_h], wi[:, o_kr + half:o_h], wi[:, o_kr:o_kr + half]],
        axis=1).astype(BF16)
    w_gate = wi[:, o_g:].astype(BF16)
    uq = mla_w_uq[l].reshape(Q_LORA, MLA_HEADS, MLA_NOPE + MLA_ROPE)
    nope, r1, r2 = uq[..., :MLA_NOPE], uq[..., MLA_NOPE:MLA_NOPE + half], uq[..., MLA_NOPE + half:]
    z_pad = jnp.zeros((Q_LORA, MLA_HEADS, HEAD_PAD - MLA_NOPE - MLA_ROPE), F32)
    z_nope = jnp.zeros((Q_LORA, MLA_HEADS, MLA_NOPE), F32)
    wuq = jnp.concatenate(
        [jnp.concatenate([nope, r1, r2, z_pad], axis=-1).reshape(Q_LORA, MLA_HEADS * HEAD_PAD),
         jnp.concatenate([z_nope, r2, r1, z_pad], axis=-1).reshape(Q_LORA, MLA_HEADS * HEAD_PAD)],
        axis=1).astype(BF16)
    uk = mla_w_uk[l].reshape(KV_LORA, MLA_HEADS, MLA_NOPE)
    wk_pad = jnp.concatenate([uk, jnp.zeros((KV_LORA, MLA_HEADS, HEAD_PAD - MLA_NOPE), F32)],
                             axis=-1).reshape(KV_LORA, MLA_HEADS * HEAD_PAD).astype(BF16)
    e_np = np.zeros((MLA_ROPE, MLA_HEADS, HEAD_PAD), np.float32)
    for r in range(MLA_ROPE):
        e_np[r, :, MLA_NOPE + r] = 1.0
    e_mat = jnp.asarray(e_np.reshape(MLA_ROPE, MLA_HEADS * HEAD_PAD), BF16)
    r2d = lambda a: a[l].reshape(1, -1)
    return dict(
        gpre=r2d(norm_mix_pre), gpost=r2d(norm_mix_post), fpre=r2d(norm_ffn_pre), fpost=r2d(norm_ffn_post),
        w_small=w_small, w_gate=w_gate, b_gate=r2d(b_gate), conva=conv_a_w[l], qn=r2d(mla_q_norm), wuq=wuq,
        kvn=r2d(mla_kv_norm), wk_pad=wk_pad, e_mat=e_mat, wvt=mla_w_uv[l].T.astype(BF16), gn=r2d(hgrn_norm),
        cdw=conv_d_w[l], cdb=r2d(conv_d_b), lng=r2d(ln_d_g), lnb=r2d(ln_d_b),
        wb=w_branch[l].astype(BF16), wo=w_out[l].astype(BF16), wup=ffn_w_up[l].astype(BF16),
        cw=ffn_conv_w[l], wdn=ffn_w_down[l].astype(BF16))


def _layer(x, tabs, past, state, w, lb_raw, layer):
    B, T, _ = x.shape
    buf_a, s0, buf_d, buf_f = state
    prompt = past is None
    tp = _tile(T, TILE_PROJ)
    tk = _tile(tp, TILE_ATTN_K)
    outs = _proj(x, w["gpre"], w["w_small"], w["conva"], w["qn"], w["wuq"], w["kvn"], w["cdw"], w["cdb"],
                 w["lng"], w["lnb"], buf_a, buf_d, tabs, w["wk_pad"], w["e_mat"], w["wvt"],
                 tp, tk if prompt else 0)
    (out_a, out_d, q, ckv, kr, hq, hf, hi, hg, nbuf_a, nbuf_d) = outs[:11]

    if prompt:
        kcat, vt = outs[11:]
        out_b = _attn(q, kcat, vt, True, T, _tile(T, TILE_ATTN_Q), tk)
    else:
        n_keys = past[0].shape[1] + T
        n_pad = -n_keys % 128
        ckv_all = jnp.concatenate([past[0], ckv, jnp.zeros((B, n_pad, KV_LORA), F32)], axis=1)
        kr_all = jnp.concatenate([past[1], kr, jnp.zeros((B, n_pad, MLA_ROPE), F32)], axis=1)
        nk = n_keys + n_pad
        kcat, vt = _kvexpand(ckv_all, kr_all, w["wk_pad"], w["e_mat"], w["wvt"], nk)
        q_pad = -T % 128
        q_in = jnp.concatenate([q, jnp.zeros((B, q_pad, q.shape[-1]), q.dtype)], axis=1)
        out_b = _attn(q_in, kcat, vt, False, n_keys, T + q_pad, nk)[:, :T]

    cc = min(HG_CHUNK, T)
    out_c, s_new = _hgrn(hq, hf, hi, hg, lb_raw, w["gn"], s0, layer, _tile(T, TILE_HGRN), cc)

    flat = lambda a: a.reshape(B * T, a.shape[-1])
    x2 = _merge(flat(x), [flat(out_a), flat(out_b), flat(out_c), flat(out_d)], w["gpre"], w["w_gate"],
                w["b_gate"], w["wb"], w["wo"], w["gpost"], _tile(B * T, TILE_MERGE))
    x3, nbuf_f = _ffn(x2.reshape(B, T, D_MODEL), w["fpre"], w["wup"], w["cw"], w["wdn"], w["fpost"], buf_f,
                      _tile(T, TILE_FFN))
    return x3, (ckv, kr, nbuf_a, s_new, nbuf_d, nbuf_f)


def kernel(x_prompt, x_sample, cache_ckv, cache_krope, state_conv_a, state_hgrn, state_conv_d, state_ffn_conv,
           norm_mix_pre, norm_mix_post, norm_ffn_pre, norm_ffn_post, w_in, b_gate, conv_a_w,
           mla_q_norm, mla_w_uq, mla_kv_norm, mla_w_uk, mla_w_uv, hgrn_lb_raw, hgrn_norm,
           conv_d_w, conv_d_b, ln_d_g, ln_d_b, w_branch, w_out, ffn_w_up, ffn_conv_w, ffn_w_down):
    depth = w_in.shape[0]
    B, T, _ = x_prompt.shape
    Bs, Ts, _ = x_sample.shape
    past_len = cache_ckv.shape[2]
    tabs_p = _rope_tables(jnp.arange(T))
    tabs_s = _rope_tables(past_len + jnp.arange(Ts))
    zero_state = (jnp.zeros((B, A_CONV - 1, BRANCH_W), F32), jnp.zeros((B, HG_HEADS, HG_D, HG_D), F32),
                  jnp.zeros((B, D_CONV - 1, BRANCH_W), F32), jnp.zeros((B, FFN_CONV - 1, D_FF), F32))
    lb_raw = hgrn_lb_raw.astype(F32)
    yp, ys = x_prompt, x_sample
    p_states, s_states = [], []
    for l in range(depth):
        w = _layer_weights(l, norm_mix_pre, norm_mix_post, norm_ffn_pre, norm_ffn_post, w_in, b_gate, conv_a_w,
                           mla_q_norm, mla_w_uq, mla_kv_norm, mla_w_uk, mla_w_uv, hgrn_norm,
                           conv_d_w, conv_d_b, ln_d_g, ln_d_b, w_branch, w_out, ffn_w_up, ffn_conv_w,
                           ffn_w_down)
        yp, sp = _layer(yp, tabs_p, None, zero_state, w, lb_raw, l)
        ys, ss = _layer(ys, tabs_s, (cache_ckv[l], cache_krope[l]),
                        (state_conv_a[l], state_hgrn[l], state_conv_d[l], state_ffn_conv[l]), w, lb_raw, l)
        p_states.append(sp)
        s_states.append(ss)
    ps = [jnp.stack(z) for z in zip(*p_states)]
    sst = [jnp.stack(z) for z in zip(*s_states)]
    return (yp, ys, ps[0], ps[1], ps[2], ps[3], ps[4], ps[5], sst[0], sst[1], sst[2], sst[3], sst[4], sst[5])
```
